```python
import math
import jax, jax.numpy as jnp
from jax import lax
import numpy as np

D_MODEL = 1024
BATCH = 32
SEQ = 256
DEPTH = 4
DEC_BATCH = 8
DEC_SEQ = 2048
PAST_LEN = 256

GRID_W = 64
N_EVEN = (DEPTH + 1) // 2
N_ODD = DEPTH // 2
N_DIR = 2
N_MOD = 6
D_FF = 4 * D_MODEL
EPS = 1e-6
POS_BASE = 10000.0
D_A = D_MODEL // 2
S5_GROUP = 16
G_A = D_A // S5_GROUP
P_A = 64
D_B = D_MODEL // 2
H_B = 4
DK_B = D_B // H_B
DV_B = D_B // H_B
CHUNK = 64
CONV_K = 4
CONV_LEFT = (CONV_K - 1) // 2
D_RNN = D_MODEL
LRU_BLOCKS = 4
LRU_BS = D_RNN // LRU_BLOCKS
LRU_C = 8.0
IN_EVEN = 2 * D_A + 4 * D_B + 2 * N_DIR * H_B
IN_ODD = 2 * D_RNN

kernel_name = 'hybrid_s5_gdn_rglru_diffusion_step'


def rms_norm(x, g):
    xf = x.astype(jnp.float32)
    y = xf * lax.rsqrt(jnp.mean(xf * xf, axis=-1, keepdims=True) + EPS)
    return y * g.astype(jnp.float32)


def modulate(h, shift, scale):
    return h * (1.0 + scale) + shift


def split_cols(x, sizes):
    out, start = [], 0
    for s in sizes:
        out.append(x[..., start:start + s])
        start += s
    return out


def l2norm(x):
    return x * lax.rsqrt(jnp.sum(x * x, axis=-1, keepdims=True) + EPS)


def grid_sincos(n_tokens):
    f32 = jnp.float32
    rows = n_tokens // GRID_W
    row = jnp.repeat(jnp.arange(rows, dtype=f32), GRID_W)
    col = jnp.tile(jnp.arange(GRID_W, dtype=f32), rows)
    n_freq = D_MODEL // 4
    omega = POS_BASE ** (-jnp.arange(n_freq, dtype=f32) / n_freq)
    ar = row[:, None] * omega
    ac = col[:, None] * omega
    return jnp.concatenate([jnp.sin(ar), jnp.cos(ar), jnp.sin(ac), jnp.cos(ac)], axis=-1)


def dwconv_centred(x, w, b):
    L = x.shape[1]
    xp = jnp.pad(x, ((0, 0), (CONV_LEFT, CONV_K - 1 - CONV_LEFT), (0, 0)))
    out = b.astype(jnp.float32)
    for j in range(CONV_K):
        out = out + xp[:, j:j + L] * w[j].astype(jnp.float32)
    return out


def _real_combine(l, r):
    a_l, b_l = l
    a_r, b_r = r
    return a_l * a_r, a_r * b_l + b_r


def linear_scan(a, b, h0, reverse):
    a_cum, h = lax.associative_scan(_real_combine, (a, b), axis=1, reverse=reverse)
    h = h + a_cum * h0[:, None]
    final = h[:, 0] if reverse else h[:, -1]
    return h, final


def _complex_combine(l, r):
    ar_l, ai_l, br_l, bi_l = l
    ar_r, ai_r, br_r, bi_r = r
    return (ar_l * ar_r - ai_l * ai_r,
            ar_l * ai_r + ai_l * ar_r,
            ar_r * br_l - ai_r * bi_l + br_r,
            ar_r * bi_l + ai_r * br_l + bi_r)


def complex_scan(a_re, a_im, b_re, b_im, h0_re, h0_im, reverse):
    A_re, A_im, h_re, h_im = lax.associative_scan(
        _complex_combine, (a_re, a_im, b_re, b_im), axis=1, reverse=reverse)
    h0r = h0_re[:, None]
    h0i = h0_im[:, None]
    h_re = h_re + A_re * h0r - A_im * h0i
    h_im = h_im + A_re * h0i + A_im * h0r
    if reverse:
        return h_re, h_im, h_re[:, 0], h_im[:, 0]
    return h_re, h_im, h_re[:, -1], h_im[:, -1]


def s5_mixer(u, z, lam_re, lam_im, log_dt, b_re, b_im, c_re, c_im, d_skip, h0_re, h0_im):
    f32 = jnp.float32
    bsz, L, _ = u.shape
    uf = u.astype(f32)
    ug = uf.reshape(bsz, L, G_A, S5_GROUP)
    bu_re = jnp.einsum('blgc,gpc->blgp', ug, b_re.astype(f32))
    bu_im = jnp.einsum('blgc,gpc->blgp', ug, b_im.astype(f32))
    y = uf * d_skip.astype(f32)
    finals_re, finals_im = [], []
    for d in range(N_DIR):
        lr = lam_re[d].astype(f32)
        li = lam_im[d].astype(f32)
        dt = jnp.exp(log_dt[d].astype(f32))[:, None]
        mag = jnp.exp(lr * dt)
        ar = mag * jnp.cos(li * dt)
        ai = mag * jnp.sin(li * dt)
        den = lr * lr + li * li
        fr = ((ar - 1.0) * lr + ai * li) / den
        fi = (ai * lr - (ar - 1.0) * li) / den
        br = fr * bu_re - fi * bu_im
        bi = fr * bu_im + fi * bu_re
        shape = br.shape
        h_re, h_im, f_re, f_im = complex_scan(
            jnp.broadcast_to(ar, shape), jnp.broadcast_to(ai, shape), br, bi,
            h0_re[:, d].astype(f32), h0_im[:, d].astype(f32), reverse=(d == 1))
        y = y + (jnp.einsum('blgp,gcp->blgc', h_re, c_re.astype(f32))
                 - jnp.einsum('blgp,gcp->blgc', h_im, c_im.astype(f32))).reshape(bsz, L, D_A)
        finals_re.append(f_re)
        finals_im.append(f_im)
    out = jax.nn.gelu(y) * jax.nn.sigmoid(z.astype(f32))
    return out, jnp.stack(finals_re, axis=1), jnp.stack(finals_im, axis=1)


def gated_delta_chunked(q, k, v, beta, g, S0):
    bsz, L, H, _ = q.shape
    dv = v.shape[-1]
    n = L // CHUNK

    def to_chunks(t):
        return t.reshape(bsz, n, CHUNK, H, -1).transpose(1, 0, 3, 2, 4)

    qc, kc, vc = to_chunks(q), to_chunks(k), to_chunks(v)
    bc = beta.reshape(bsz, n, CHUNK, H).transpose(1, 0, 3, 2)
    gc = jnp.cumsum(g.reshape(bsz, n, CHUNK, H).transpose(1, 0, 3, 2), axis=-1)
    idx = jnp.arange(CHUNK)
    incl = idx[:, None] >= idx[None, :]
    strict = idx[:, None] > idx[None, :]
    decay = jnp.exp(jnp.where(incl, gc[..., :, None] - gc[..., None, :], -jnp.inf))
    kb = kc * bc[..., None]
    vb = vc * bc[..., None]
    lmat = jnp.where(strict, jnp.einsum('nbhcd,nbhed->nbhce', kb, kc) * decay, 0.0)
    a_mat = lmat + jnp.eye(CHUNK, dtype=lmat.dtype)
    rhs = jnp.concatenate([vb, kb * jnp.exp(gc)[..., None]], axis=-1)
    sol = lax.linalg.triangular_solve(a_mat, rhs, left_side=True, lower=True, unit_diagonal=True)
    u_c, w_c = sol[..., :dv], sol[..., dv:]
    qk = jnp.where(incl, jnp.einsum('nbhcd,nbhed->nbhce', qc, kc) * decay, 0.0)

    def step(S, xs):
        q_i, k_i, u_i, w_i, g_i, qk_i = xs
        v_new = u_i - jnp.einsum('bhck,bhkv->bhcv', w_i, S)
        o_i = (jnp.einsum('bhck,bhkv->bhcv', q_i * jnp.exp(g_i)[..., None], S)
               + jnp.einsum('bhce,bhev->bhcv', qk_i, v_new))
        g_last = g_i[..., -1:]
        S = (S * jnp.exp(g_last)[..., None]
             + jnp.einsum('bhck,bhcv->bhkv', k_i * jnp.exp(g_last - g_i)[..., None], v_new))
        return S, o_i

    S_fin, o = lax.scan(step, S0, (qc, kc, u_c, w_c, gc, qk))
    o = o.transpose(1, 0, 3, 2, 4).reshape(bsz, L, H, dv)
    return o, S_fin


def gdn_mixer(q, k, v, z, a_raw, b_raw, conv_w, conv_b, a_log, dt_bias, o_norm, S0):
    f32 = jnp.float32
    bsz, L, _ = q.shape
    qkv = jax.nn.silu(dwconv_centred(jnp.concatenate([q, k, v], axis=-1).astype(f32), conv_w, conv_b))
    qh, kh, vh = split_cols(qkv, (D_B, D_B, D_B))
    qh = l2norm(qh.reshape(bsz, L, H_B, DK_B)) * (DK_B ** -0.5)
    kh = l2norm(kh.reshape(bsz, L, H_B, DK_B))
    vh = vh.reshape(bsz, L, H_B, DV_B)
    a_raw = a_raw.astype(f32).reshape(bsz, L, N_DIR, H_B)
    b_raw = b_raw.astype(f32).reshape(bsz, L, N_DIR, H_B)
    o = 0.0
    finals = []
    for d in range(N_DIR):
        beta = jax.nn.sigmoid(b_raw[:, :, d])
        g = -jnp.exp(a_log[d].astype(f32)) * jax.nn.softplus(a_raw[:, :, d] + dt_bias[d].astype(f32))
        s0 = S0[:, d].astype(f32)
        if d == 0:
            o_d, s_d = gated_delta_chunked(qh, kh, vh, beta, g, s0)
        else:
            rev = lambda t: jnp.flip(t, axis=1)
            o_d, s_d = gated_delta_chunked(rev(qh), rev(kh), rev(vh), rev(beta), rev(g), s0)
            o_d = rev(o_d)
        o = o + o_d
        finals.append(s_d)
    o = rms_norm(o, o_norm) * jax.nn.silu(z.astype(f32).reshape(bsz, L, H_B, DV_B))
    return o.reshape(bsz, L, D_B), jnp.stack(finals, axis=1)


def rglru_mixer(xb, yb, conv_w, conv_b, w_r, b_r, w_i, b_i, lam, h0):
    f32 = jnp.float32
    bsz, L, _ = xb.shape
    x = dwconv_centred(xb.astype(f32), conv_w, conv_b)
    xblk = x.reshape(bsz, L, LRU_BLOCKS, LRU_BS)
    h_sum = 0.0
    finals = []
    for d in range(N_DIR):
        r = jax.nn.sigmoid(jnp.einsum('blni,nij->blnj', xblk, w_r[d].astype(f32)).reshape(bsz, L, D_RNN)
                           + b_r[d].astype(f32))
        i = jax.nn.sigmoid(jnp.einsum('blni,nij->blnj', xblk, w_i[d].astype(f32)).reshape(bsz, L, D_RNN)
                           + b_i[d].astype(f32))
        log_a = -LRU_C * r * jax.nn.softplus(-lam[d].astype(f32))
        a = jnp.exp(log_a)
        b = jnp.sqrt(-jnp.expm1(2.0 * log_a)) * (i * x)
        h, h_fin = linear_scan(a, b, h0[:, d].astype(f32), reverse=(d == 1))
        h_sum = h_sum + h
        finals.append(h_fin)
    return h_sum * jax.nn.gelu(yb.astype(f32)), jnp.stack(finals, axis=1)


def trunk(x, mod, s5_re0, s5_im0, delta0, lru0, p):
    dtype = x.dtype
    fin_re, fin_im, fin_delta, fin_lru = [], [], [], []
    for l in range(DEPTH):
        sh_m, sc_m, gt_m, sh_f, sc_f, gt_f = jnp.split(mod[l][:, None, :], N_MOD, axis=-1)
        h = modulate(rms_norm(x, p['norm_mix_pre'][l]), sh_m, sc_m)
        if l % 2 == 0:
            e = l // 2
            u_a, z_a, q, k, v, z_b, a_raw, b_raw = split_cols(
                h @ p['w_in_even'][e], (D_A, D_A, D_B, D_B, D_B, D_B, N_DIR * H_B, N_DIR * H_B))
            y_a, f_re, f_im = s5_mixer(u_a, z_a, p['s5_lam_re'][e], p['s5_lam_im'][e], p['s5_log_dt'][e],
                                       p['s5_b_re'][e], p['s5_b_im'][e], p['s5_c_re'][e], p['s5_c_im'][e],
                                       p['s5_d'][e], s5_re0[:, e], s5_im0[:, e])
            y_b, f_d = gdn_mixer(q, k, v, z_b, a_raw, b_raw, p['gdn_conv_w'][e], p['gdn_conv_b'][e],
                                 p['gdn_a_log'][e], p['gdn_dt_bias'][e], p['gdn_o_norm'][e], delta0[:, e])
            out = jnp.concatenate([y_a, y_b], axis=-1) @ p['w_out_even'][e]
            fin_re.append(f_re)
            fin_im.append(f_im)
            fin_delta.append(f_d)
        else:
            o = l // 2
            x_b, y_g = split_cols(h @ p['w_in_odd'][o], (D_RNN, D_RNN))
            y_c, f_l = rglru_mixer(x_b, y_g, p['lru_conv_w'][o], p['lru_conv_b'][o], p['lru_w_r'][o],
                                   p['lru_b_r'][o], p['lru_w_i'][o], p['lru_b_i'][o], p['lru_lam'][o],
                                   lru0[:, o])
            out = y_c @ p['w_out_odd'][o]
            fin_lru.append(f_l)
        x = (x + gt_m * rms_norm(out, p['norm_mix_post'][l])).astype(dtype)
        h = modulate(rms_norm(x, p['norm_mlp_pre'][l]), sh_f, sc_f)
        f = jnp.square(jax.nn.relu(h @ p['w_mlp_in'][l])) @ p['w_mlp_out'][l]
        x = (x + gt_f * rms_norm(f, p['norm_mlp_post'][l])).astype(dtype)
    return (x, jnp.stack(fin_re, axis=1), jnp.stack(fin_im, axis=1),
            jnp.stack(fin_delta, axis=1), jnp.stack(fin_lru, axis=1))


def setup_inputs(seed: int = 0) -> dict:
    key = jax.random.key(seed)
    ks = iter(jax.random.split(key, 64))
    f32 = jnp.float32

    def normal(shape, scale):
        return jax.random.normal(next(ks), shape, f32) * scale

    def gain(shape):
        return 1.0 + normal(shape, 0.02)

    def inv_softplus_dt(shape):
        dt = jnp.exp(jax.random.uniform(next(ks), shape, f32, math.log(1e-3), math.log(1e-1)))
        return dt + jnp.log(-jnp.expm1(-dt))

    x_prompt = normal((BATCH, SEQ, D_MODEL), 1.0)
    x_sample = normal((DEC_BATCH, DEC_SEQ, D_MODEL), 1.0)
    state_s5_re = normal((DEC_BATCH, N_EVEN, N_DIR, G_A, P_A), 0.1)
    state_s5_im = normal((DEC_BATCH, N_EVEN, N_DIR, G_A, P_A), 0.1)
    state_delta = normal((DEC_BATCH, N_EVEN, N_DIR, H_B, DK_B, DV_B), 0.05)
    state_lru = normal((DEC_BATCH, N_ODD, N_DIR, D_RNN), 0.5)
    c = normal((DEC_BATCH, D_MODEL), 1.0)
    c_ctx = normal((D_MODEL,), 1.0)
    w_ada = normal((DEPTH, D_MODEL, N_MOD * D_MODEL), 0.5 * D_MODEL ** -0.5)
    b_ada = normal((DEPTH, N_MOD * D_MODEL), 0.02)
    norm_mix_pre = gain((DEPTH, D_MODEL))
    norm_mix_post = gain((DEPTH, D_MODEL))
    norm_mlp_pre = gain((DEPTH, D_MODEL))
    norm_mlp_post = gain((DEPTH, D_MODEL))
    w_mlp_in = normal((DEPTH, D_MODEL, D_FF), D_MODEL ** -0.5)
    w_mlp_out = normal((DEPTH, D_FF, D_MODEL), D_FF ** -0.5)
    w_in_even = normal((N_EVEN, D_MODEL, IN_EVEN), D_MODEL ** -0.5)
    w_out_even = normal((N_EVEN, D_A + D_B, D_MODEL), (D_A + D_B) ** -0.5)
    s5_lam_re = -0.5 + normal((N_EVEN, N_DIR, G_A, P_A), 0.01)
    s5_lam_im = jnp.pi * jnp.arange(P_A, dtype=f32) + normal((N_EVEN, N_DIR, G_A, P_A), 0.01)
    s5_log_dt = jax.random.uniform(next(ks), (N_EVEN, N_DIR, G_A), f32, math.log(1e-3), math.log(1e-1))
    s5_b_re = normal((N_EVEN, G_A, P_A, S5_GROUP), (2.0 * S5_GROUP) ** -0.5)
    s5_b_im = normal((N_EVEN, G_A, P_A, S5_GROUP), (2.0 * S5_GROUP) ** -0.5)
    s5_c_re = normal((N_EVEN, G_A, S5_GROUP, P_A), (2.0 * P_A) ** -0.5)
    s5_c_im = normal((N_EVEN, G_A, S5_GROUP, P_A), (2.0 * P_A) ** -0.5)
    s5_d = normal((N_EVEN, D_A), 1.0)
    gdn_conv_w = normal((N_EVEN, CONV_K, 3 * D_B), CONV_K ** -0.5)
    gdn_conv_b = normal((N_EVEN, 3 * D_B), 0.02)
    gdn_a_log = jnp.log(jax.random.uniform(next(ks), (N_EVEN, N_DIR, H_B), f32, 1.0, 16.0))
    gdn_dt_bias = inv_softplus_dt((N_EVEN, N_DIR, H_B))
    gdn_o_norm = gain((N_EVEN, DV_B))
    w_in_odd = normal((N_ODD, D_MODEL, IN_ODD), D_MODEL ** -0.5)
    w_out_odd = normal((N_ODD, D_RNN, D_MODEL), D_RNN ** -0.5)
    lru_conv_w = normal((N_ODD, CONV_K, D_RNN), CONV_K ** -0.5)
    lru_conv_b = normal((N_ODD, D_RNN), 0.02)
    lru_w_r = normal((N_ODD, N_DIR, LRU_BLOCKS, LRU_BS, LRU_BS), LRU_BS ** -0.5)
    lru_b_r = normal((N_ODD, N_DIR, D_RNN), 0.02)
    lru_w_i = normal((N_ODD, N_DIR, LRU_BLOCKS, LRU_BS, LRU_BS), LRU_BS ** -0.5)
    lru_b_i = normal((N_ODD, N_DIR, D_RNN), 0.02)
    a0 = jax.random.uniform(next(ks), (N_ODD, N_DIR, D_RNN), f32, 0.9, 0.999)
    s = a0 ** (1.0 / LRU_C)
    lru_lam = jnp.log(s) - jnp.log1p(-s)
    return {
        'x_prompt': x_prompt, 'x_sample': x_sample,
        'state_s5_re': state_s5_re, 'state_s5_im': state_s5_im,
        'state_delta': state_delta, 'state_lru': state_lru,
        'c': c, 'c_ctx': c_ctx,
        'w_ada': w_ada, 'b_ada': b_ada,
        'norm_mix_pre': norm_mix_pre, 'norm_mix_post': norm_mix_post,
        'norm_mlp_pre': norm_mlp_pre, 'norm_mlp_post': norm_mlp_post,
        'w_mlp_in': w_mlp_in, 'w_mlp_out': w_mlp_out,
        'w_in_even': w_in_even, 'w_out_even': w_out_even,
        's5_lam_re': s5_lam_re, 's5_lam_im': s5_lam_im, 's5_log_dt': s5_log_dt,
        's5_b_re': s5_b_re, 's5_b_im': s5_b_im, 's5_c_re': s5_c_re, 's5_c_im': s5_c_im, 's5_d': s5_d,
        'gdn_conv_w': gdn_conv_w, 'gdn_conv_b': gdn_conv_b, 'gdn_a_log': gdn_a_log,
        'gdn_dt_bias': gdn_dt_bias, 'gdn_o_norm': gdn_o_norm,
        'w_in_odd': w_in_odd, 'w_out_odd': w_out_odd,
        'lru_conv_w': lru_conv_w, 'lru_conv_b': lru_conv_b,
        'lru_w_r': lru_w_r, 'lru_b_r': lru_b_r, 'lru_w_i': lru_w_i, 'lru_b_i': lru_b_i,
        'lru_lam': lru_lam,
    }


def reference(x_prompt, x_sample, state_s5_re, state_s5_im, state_delta, state_lru, c, c_ctx,
              w_ada, b_ada, norm_mix_pre, norm_mix_post, norm_mlp_pre, norm_mlp_post,
              w_mlp_in, w_mlp_out, w_in_even, w_out_even,
              s5_lam_re, s5_lam_im, s5_log_dt, s5_b_re, s5_b_im, s5_c_re, s5_c_im, s5_d,
              gdn_conv_w, gdn_conv_b, gdn_a_log, gdn_dt_bias, gdn_o_norm,
              w_in_odd, w_out_odd, lru_conv_w, lru_conv_b, lru_w_r, lru_b_r, lru_w_i, lru_b_i,
              lru_lam):
    f32 = jnp.float32
    p = dict(norm_mix_pre=norm_mix_pre, norm_mix_post=norm_mix_post,
             norm_mlp_pre=norm_mlp_pre, norm_mlp_post=norm_mlp_post,
             w_mlp_in=w_mlp_in, w_mlp_out=w_mlp_out, w_in_even=w_in_even, w_out_even=w_out_even,
             s5_lam_re=s5_lam_re, s5_lam_im=s5_lam_im, s5_log_dt=s5_log_dt,
             s5_b_re=s5_b_re, s5_b_im=s5_b_im, s5_c_re=s5_c_re, s5_c_im=s5_c_im, s5_d=s5_d,
             gdn_conv_w=gdn_conv_w, gdn_conv_b=gdn_conv_b, gdn_a_log=gdn_a_log,
             gdn_dt_bias=gdn_dt_bias, gdn_o_norm=gdn_o_norm,
             w_in_odd=w_in_odd, w_out_odd=w_out_odd, lru_conv_w=lru_conv_w, lru_conv_b=lru_conv_b,
             lru_w_r=lru_w_r, lru_b_r=lru_b_r, lru_w_i=lru_w_i, lru_b_i=lru_b_i, lru_lam=lru_lam)
    mod_ctx = (jnp.einsum('d,lde->le', jax.nn.silu(c_ctx.astype(f32)), w_ada) + b_ada.astype(f32))[:, None, :]
    mod_lat = jnp.einsum('bd,lde->lbe', jax.nn.silu(c.astype(f32)), w_ada) + b_ada.astype(f32)[:, None, :]

    bp = x_prompt.shape[0]
    zero_re = jnp.zeros((bp, N_EVEN, N_DIR, G_A, P_A), f32)
    zero_delta = jnp.zeros((bp, N_EVEN, N_DIR, H_B, DK_B, DV_B), f32)
    zero_lru = jnp.zeros((bp, N_ODD, N_DIR, D_RNN), f32)
    y_prompt, new_s5_re, new_s5_im, new_delta, new_lru = trunk(
        x_prompt, mod_ctx, zero_re, zero_re, zero_delta, zero_lru, p)

    x_lat = (x_sample.astype(f32) + grid_sincos(x_sample.shape[1])[None]).astype(x_sample.dtype)
    y_sample = trunk(x_lat, mod_lat, state_s5_re, state_s5_im, state_delta, state_lru, p)[0]
    return (y_prompt, y_sample, new_s5_re, new_s5_im, new_delta, new_lru)
```

```python
import functools
from typing import NamedTuple

import jax
import jax.numpy as jnp
from jax import lax
from jax.experimental import pallas as pl
from jax.experimental.pallas import tpu as pltpu

F32 = jnp.float32
BF16 = jnp.bfloat16
HIGHEST = lax.Precision.HIGHEST

EPS = 1e-6
POS_BASE = 10000.0
GRID_W = 64
N_MOD = 6
S5_GROUP = 16
S5_STATES = 64
GDN_HEADS = 4
GDN_CHUNK = 64
CONV_K = 4
CONV_LEFT = (CONV_K - 1) // 2
LRU_BLOCKS = 4
LRU_C = 8.0

SUBLANES = 8
LANES = 128
SEQ_GROUP = SUBLANES
SCAN_STEPS = 32
SCAN_ROWS = SCAN_STEPS * SEQ_GROUP
GDN_ROWS = 256
TOKEN_ROWS = 512
S5_SLICE = 128
VMEM_LIMIT_BYTES = 56 * 1024 * 1024


class Layout(NamedTuple):
    n_ctx: int
    len_ctx: int
    n_lat: int
    len_lat: int

    @property
    def ctx_rows(self):
        return self.n_ctx * self.len_ctx

    @property
    def rows(self):
        return self.ctx_rows + self.n_lat * self.len_lat

    @property
    def groups(self):
        return self.n_ctx + self.n_lat

    def tiles(self, tile_rows):
        return self.ctx_rows // tile_rows + self.n_lat * (self.len_lat // tile_rows)


def _schedule(lay, tile_rows, reverse, i):
    nc = lay.len_ctx // tile_rows
    nl = lay.len_lat // tile_rows
    n_ctx_tiles = lay.n_ctx * nc
    is_ctx = i < n_ctx_tiles
    j = jnp.maximum(i - n_ctx_tiles, 0)
    ic = jnp.minimum(i, n_ctx_tiles - 1)
    g = jnp.where(is_ctx, ic // nc, lay.n_ctx + j // nl)
    step = jnp.where(is_ctx, ic % nc, j % nl)
    n = jnp.where(is_ctx, nc, nl)
    k = (n - 1 - step) if reverse else step
    base = jnp.where(is_ctx, (ic // nc) * nc, n_ctx_tiles + (j // nl) * nl)
    return base + k, g, step == 0, step == n - 1, k, n


def _params(n_axes=1):
    return pltpu.CompilerParams(dimension_semantics=("arbitrary",) * n_axes,
                                vmem_limit_bytes=VMEM_LIMIT_BYTES)


def _dot(a, b):
    return jnp.dot(a, b, preferred_element_type=F32)


def _dot_hi(a, b):
    return jnp.dot(a, b, precision=HIGHEST, preferred_element_type=F32)


def _dot_nt_hi(a, b):
    return lax.dot_general(a, b, (((1,), (1,)), ((), ())), precision=HIGHEST,
                           preferred_element_type=F32)


def _dot_tn(a, b):
    return lax.dot_general(a, b, (((0,), (0,)), ((), ())), preferred_element_type=F32)


def _rms(x):
    return x * lax.rsqrt(jnp.mean(x * x, axis=-1, keepdims=True) + EPS)


def _per_group(rows_val, vec8):
    r, c = rows_val.shape
    return rows_val.reshape(r // SEQ_GROUP, SEQ_GROUP, c), vec8[None]


def _modulate(h, shift8, scale8):
    h3, sc = _per_group(h, scale8)
    out = h3 * (1.0 + sc) + shift8[None]
    return out.reshape(h.shape)


def _gated_residual(x, y, gate8):
    y3, g = _per_group(y, gate8)
    return x + (y3 * g).reshape(x.shape)


def _mod_chunk(mod_ref, idx, d):
    return mod_ref[0, :, idx * d:(idx + 1) * d]


def _softplus(x):
    return jnp.maximum(x, 0.0) + jnp.log1p(jnp.exp(-jnp.abs(x)))


def _ada_kernel(c_ref, w_ref, b_ref, o_ref):
    s = jax.nn.silu(c_ref[...]).astype(BF16)
    o_ref[0] = _dot(s, w_ref[0].astype(BF16)) + b_ref[0]


def _ada_vectors(cond, w_ada, b_ada):
    depth, d, n = w_ada.shape
    rows = cond.shape[0]
    tn = 512
    return pl.pallas_call(
        _ada_kernel,
        grid=(depth, n // tn),
        in_specs=[pl.BlockSpec((rows, d), lambda l, j: (0, 0)),
                  pl.BlockSpec((1, d, tn), lambda l, j: (l, 0, j)),
                  pl.BlockSpec((1, 1, tn), lambda l, j: (l, 0, j))],
        out_specs=pl.BlockSpec((1, rows, tn), lambda l, j: (l, 0, j)),
        out_shape=jax.ShapeDtypeStruct((depth, rows, n), F32),
        compiler_params=_params(2),
    )(cond, w_ada, b_ada.reshape(depth, 1, n))


def _pos_kernel(x_ref, p_ref, o_ref):
    o_ref[0] = x_ref[0] + p_ref[...]


def _grid_sincos_table(n_tokens, d_model):
    rows = n_tokens // GRID_W
    row = jnp.repeat(jnp.arange(rows, dtype=F32), GRID_W)
    col = jnp.tile(jnp.arange(GRID_W, dtype=F32), rows)
    n_freq = d_model // 4
    omega = POS_BASE ** (-jnp.arange(n_freq, dtype=F32) / n_freq)
    ar = row[:, None] * omega
    ac = col[:, None] * omega
    return jnp.concatenate([jnp.sin(ar), jnp.cos(ar), jnp.sin(ac), jnp.cos(ac)], axis=-1)


def _add_positions(x, table):
    b, t, d = x.shape
    tt = min(t, 512)
    return pl.pallas_call(
        _pos_kernel,
        grid=(b, t // tt),
        in_specs=[pl.BlockSpec((1, tt, d), lambda i, j: (i, j, 0)),
                  pl.BlockSpec((tt, d), lambda i, j: (j, 0))],
        out_specs=pl.BlockSpec((1, tt, d), lambda i, j: (i, j, 0)),
        out_shape=jax.ShapeDtypeStruct(x.shape, x.dtype),
        compiler_params=_params(2),
    )(x, table)


def _mod_index(lay, tile_rows):
    def index(i):
        row = i * tile_rows
        lat = jnp.maximum(row - lay.ctx_rows, 0) // lay.len_lat
        return jnp.where(row < lay.ctx_rows, 0, 1 + lat)
    return index


def _pre_kernel(x_ref, mod_ref, gain_ref, w_ref, *out_refs, splits):
    d = x_ref.shape[-1]
    h = _rms(x_ref[...]) * gain_ref[...]
    h = _modulate(h, _mod_chunk(mod_ref, 0, d), _mod_chunk(mod_ref, 1, d)).astype(BF16)
    for (a, b), o_ref in zip(splits, out_refs):
        o_ref[...] = _dot(h, w_ref[:, a:b])


def _pre_mixer(lay, x, mod, gain, w, splits):
    rows, d = x.shape
    n = w.shape[1]
    midx = _mod_index(lay, TOKEN_ROWS)
    return pl.pallas_call(
        functools.partial(_pre_kernel, splits=splits),
        grid=(rows // TOKEN_ROWS,),
        in_specs=[pl.BlockSpec((TOKEN_ROWS, d), lambda i: (i, 0)),
                  pl.BlockSpec((1, SEQ_GROUP, N_MOD * d), lambda i: (midx(i), 0, 0)),
                  pl.BlockSpec((1, d), lambda i: (0, 0)),
                  pl.BlockSpec((d, n), lambda i: (0, 0))],
        out_specs=[pl.BlockSpec((TOKEN_ROWS, b - a), lambda i: (i, 0)) for a, b in splits],
        out_shape=[jax.ShapeDtypeStruct((rows, b - a), F32) for a, b in splits],
        compiler_params=_params(),
    )(x, mod, gain, w)


def _finish_sublayer(x, out, gain, gate8):
    return _gated_residual(x, _rms(out) * gain, gate8)


def _post_odd_kernel(hf_ref, hb_ref, yg_ref, x_ref, mod_ref, gain_ref, w_ref, o_ref):
    d = x_ref.shape[-1]
    y = (hf_ref[...] + hb_ref[...]) * jax.nn.gelu(yg_ref[...])
    out = _dot(y.astype(BF16), w_ref[...])
    o_ref[...] = _finish_sublayer(x_ref[...], out, gain_ref[...], _mod_chunk(mod_ref, 2, d))


def _post_odd(lay, hf, hb, yg, x, mod, gain, w):
    rows, d = x.shape
    dr = hf.shape[1]
    midx = _mod_index(lay, TOKEN_ROWS)
    row_spec = lambda c: pl.BlockSpec((TOKEN_ROWS, c), lambda i: (i, 0))
    return pl.pallas_call(
        _post_odd_kernel,
        grid=(rows // TOKEN_ROWS,),
        in_specs=[row_spec(dr), row_spec(dr), row_spec(dr), row_spec(d),
                  pl.BlockSpec((1, SEQ_GROUP, N_MOD * d), lambda i: (midx(i), 0, 0)),
                  pl.BlockSpec((1, d), lambda i: (0, 0)),
                  pl.BlockSpec((dr, d), lambda i: (0, 0))],
        out_specs=row_spec(d),
        out_shape=jax.ShapeDtypeStruct((rows, d), F32),
        compiler_params=_params(),
    )(hf, hb, yg, x, mod, gain, w)


def _post_even_kernel(u_ref, za_ref, yf_ref, yb_ref, of_ref, ob_ref, zb_ref, x_ref, mod_ref,
                      dskip_ref, onorm_ref, gain_ref, w_ref, o_ref):
    d = x_ref.shape[-1]
    da = u_ref.shape[-1]
    db = of_ref.shape[-1]
    dv = db // GDN_HEADS
    ya = jax.nn.gelu(yf_ref[...] + yb_ref[...] + u_ref[...] * dskip_ref[...])
    ya = ya * jax.nn.sigmoid(za_ref[...])
    out = _dot(ya.astype(BF16), w_ref[0:da, :])
    o = of_ref[...] + ob_ref[...]
    zb = zb_ref[...]
    for h in range(GDN_HEADS):
        sl = slice(h * dv, (h + 1) * dv)
        oh = _rms(o[:, sl]) * onorm_ref[...] * jax.nn.silu(zb[:, sl])
        out = out + _dot(oh.astype(BF16), w_ref[da + h * dv:da + (h + 1) * dv, :])
    o_ref[...] = _finish_sublayer(x_ref[...], out, gain_ref[...], _mod_chunk(mod_ref, 2, d))


def _post_even(lay, u, za, yf, yb, of, ob, zb, x, mod, dskip, onorm, gain, w):
    rows, d = x.shape
    da, db = u.shape[1], of.shape[1]
    midx = _mod_index(lay, TOKEN_ROWS)
    row_spec = lambda c: pl.BlockSpec((TOKEN_ROWS, c), lambda i: (i, 0))
    const = lambda shape: pl.BlockSpec(shape, lambda i: (0, 0))
    return pl.pallas_call(
        _post_even_kernel,
        grid=(rows // TOKEN_ROWS,),
        in_specs=[row_spec(da), row_spec(da), row_spec(da), row_spec(da),
                  row_spec(db), row_spec(db), row_spec(db), row_spec(d),
                  pl.BlockSpec((1, SEQ_GROUP, N_MOD * d), lambda i: (midx(i), 0, 0)),
                  const((1, da)), const((1, db // GDN_HEADS)), const((1, d)), const((da + db, d))],
        out_specs=row_spec(d),
        out_shape=jax.ShapeDtypeStruct((rows, d), F32),
        compiler_params=_params(),
    )(u, za, yf, yb, of, ob, zb, x, mod, dskip, onorm, gain, w)


def _mlp_kernel(x_ref, mod_ref, gpre_ref, gpost_ref, w1_ref, w2_ref, o_ref):
    d = x_ref.shape[-1]
    x = x_ref[...]
    h = _rms(x) * gpre_ref[...]
    h = _modulate(h, _mod_chunk(mod_ref, 3, d), _mod_chunk(mod_ref, 4, d)).astype(BF16)
    f = jnp.square(jnp.maximum(_dot(h, w1_ref[...]), 0.0))
    out = _dot(f.astype(BF16), w2_ref[...])
    o_ref[...] = _finish_sublayer(x, out, gpost_ref[...], _mod_chunk(mod_ref, 5, d))


def _mlp(lay, x, mod, gpre, gpost, w1, w2):
    rows, d = x.shape
    dff = w1.shape[1]
    midx = _mod_index(lay, TOKEN_ROWS)
    resident = lambda shape: pl.BlockSpec(shape, lambda i: (0, 0), pipeline_mode=pl.Buffered(1))
    return pl.pallas_call(
        _mlp_kernel,
        grid=(rows // TOKEN_ROWS,),
        in_specs=[pl.BlockSpec((TOKEN_ROWS, d), lambda i: (i, 0)),
                  pl.BlockSpec((1, SEQ_GROUP, N_MOD * d), lambda i: (midx(i), 0, 0)),
                  pl.BlockSpec((1, d), lambda i: (0, 0)),
                  pl.BlockSpec((1, d), lambda i: (0, 0)),
                  resident((d, dff)), resident((dff, d))],
        out_specs=pl.BlockSpec((TOKEN_ROWS, d), lambda i: (i, 0)),
        out_shape=jax.ShapeDtypeStruct((rows, d), F32),
        compiler_params=_params(),
    )(x, mod, gpre, gpost, w1, w2)


def _s5_discretise_kernel(lr_ref, li_ref, ldt_ref, bre_ref, bim_ref,
                          ar_ref, ai_ref, bbre_ref, bbim_ref):
    lr, li = lr_ref[...], li_ref[...]
    dt = jnp.exp(ldt_ref[...])
    mag = jnp.exp(lr * dt)
    ar = mag * jnp.cos(li * dt)
    ai = mag * jnp.sin(li * dt)
    den = lr * lr + li * li
    fr = ((ar - 1.0) * lr + ai * li) / den
    fi = (ai * lr - (ar - 1.0) * li) / den
    ar_ref[...] = ar
    ai_ref[...] = ai
    bbre_ref[...] = fr * bre_ref[...] - fi * bim_ref[...]
    bbim_ref[...] = fr * bim_ref[...] + fi * bre_ref[...]


def _s5_discretise(lam_re, lam_im, log_dt, b_re, b_im):
    e, nd, g, p = lam_re.shape
    c = b_re.shape[-1]
    full = (e, nd, g, c, p)
    flat = lambda a: jnp.broadcast_to(a, full).reshape(-1, p)
    lr = flat(lam_re[:, :, :, None, :])
    li = flat(lam_im[:, :, :, None, :])
    ldt = flat(log_dt[:, :, :, None, None])
    bre = flat(b_re.transpose(0, 1, 3, 2)[:, None])
    bim = flat(b_im.transpose(0, 1, 3, 2)[:, None])
    shape = jax.ShapeDtypeStruct(lr.shape, F32)
    ar, ai, bbre, bbim = pl.pallas_call(
        _s5_discretise_kernel, out_shape=[shape] * 4,
        compiler_params=pltpu.CompilerParams(vmem_limit_bytes=VMEM_LIMIT_BYTES),
    )(lr, li, ldt, bre, bim)
    ar = ar.reshape(full)[:, :, :, 0, :]
    ai = ai.reshape(full)[:, :, :, 0, :]
    return ar, ai, bbre.reshape(full), bbim.reshape(full)


def _block_diag_in(bb):
    g, c, p = bb.shape
    per = S5_SLICE // c
    n = g // per
    blocks = bb.reshape(n, per, c, p)
    eye = jnp.eye(per, dtype=bb.dtype)
    return jnp.einsum('nicp,ij->nicjp', blocks, eye).reshape(n, per * c, per * p)


def _block_diag_out(cc):
    g, c, p = cc.shape
    per = S5_SLICE // c
    n = g // per
    blocks = cc.reshape(n, per, c, p)
    eye = jnp.eye(per, dtype=cc.dtype)
    return jnp.einsum('nicp,ij->nipjc', blocks, eye).reshape(n, per * p, per * c)


def _s5_scan_kernel(u_ref, h0re_ref, h0im_ref, a_ref, bw_ref, cw_ref,
                    y_ref, fre_ref, fim_ref, hre_ref, him_ref, bre_ref, bim_ref,
                    *, lay, reverse):
    _, _, first, last, _, _ = _schedule(lay, SCAN_ROWS, reverse, pl.program_id(0))
    n_slices = bw_ref.shape[0]
    wide = bw_ref.shape[2] // 2

    @pl.when(first)
    def _():
        hre_ref[...] = h0re_ref[0]
        him_ref[...] = h0im_ref[0]

    for j in range(n_slices):
        uj = u_ref[:, j * S5_SLICE:(j + 1) * S5_SLICE].astype(BF16)
        bb = _dot(uj, bw_ref[j])
        bre_ref[:, j * wide:(j + 1) * wide] = bb[:, :wide]
        bim_ref[:, j * wide:(j + 1) * wide] = bb[:, wide:]

    steps = range(SCAN_STEPS - 1, -1, -1) if reverse else range(SCAN_STEPS)
    for j in range(n_slices):
        cols = slice(j * wide, (j + 1) * wide)
        ar = a_ref[0, :, cols]
        ai = a_ref[1, :, cols]
        hr = hre_ref[:, cols]
        hi = him_ref[:, cols]
        for t in steps:
            rows = slice(t * SEQ_GROUP, (t + 1) * SEQ_GROUP)
            nr = ar * hr - ai * hi + bre_ref[rows, cols]
            ni = ar * hi + ai * hr + bim_ref[rows, cols]
            bre_ref[rows, cols] = nr
            bim_ref[rows, cols] = ni
            hr, hi = nr, ni
        hre_ref[:, cols] = hr
        him_ref[:, cols] = hi

    for j in range(n_slices):
        cols = slice(j * wide, (j + 1) * wide)
        yj = _dot(bre_ref[:, cols].astype(BF16), cw_ref[0, j])
        yj = yj - _dot(bim_ref[:, cols].astype(BF16), cw_ref[1, j])
        y_ref[:, j * S5_SLICE:(j + 1) * S5_SLICE] = yj

    @pl.when(last)
    def _():
        fre_ref[0] = hre_ref[...]
        fim_ref[0] = him_ref[...]


def _s5_scan(lay, u, h0re, h0im, a8, bw, cw, reverse):
    rows, da = u.shape
    ns = h0re.shape[-1]
    sched = functools.partial(_schedule, lay, SCAN_ROWS, reverse)
    tile = lambda i: (sched(i)[0], 0)
    grp = lambda i: (sched(i)[1], 0, 0)
    state_shape = jax.ShapeDtypeStruct(h0re.shape, F32)
    return pl.pallas_call(
        functools.partial(_s5_scan_kernel, lay=lay, reverse=reverse),
        grid=(lay.tiles(SCAN_ROWS),),
        in_specs=[pl.BlockSpec((SCAN_ROWS, da), tile),
                  pl.BlockSpec((1, SEQ_GROUP, ns), grp),
                  pl.BlockSpec((1, SEQ_GROUP, ns), grp),
                  pl.BlockSpec(a8.shape, lambda i: (0, 0, 0)),
                  pl.BlockSpec(bw.shape, lambda i: (0, 0, 0)),
                  pl.BlockSpec(cw.shape, lambda i: (0, 0, 0, 0))],
        out_specs=[pl.BlockSpec((SCAN_ROWS, da), tile),
                   pl.BlockSpec((1, SEQ_GROUP, ns), grp),
                   pl.BlockSpec((1, SEQ_GROUP, ns), grp)],
        out_shape=[jax.ShapeDtypeStruct((rows, da), F32), state_shape, state_shape],
        scratch_shapes=[pltpu.VMEM((SEQ_GROUP, ns), F32), pltpu.VMEM((SEQ_GROUP, ns), F32),
                        pltpu.VMEM((SCAN_ROWS, ns), F32), pltpu.VMEM((SCAN_ROWS, ns), F32)],
        compiler_params=_params(),
    )(u, h0re, h0im, a8, bw, cw)


def _halo_specs(lay, channels, reverse=False):
    sched = functools.partial(_schedule, lay, SCAN_ROWS, reverse)
    last_pair = lay.rows // (2 * SEQ_GROUP) - 1

    def prev(i):
        t = sched(i)[0]
        return (jnp.maximum(t * SCAN_STEPS - 1, 0), 0)

    def nxt(i):
        t = sched(i)[0]
        return (jnp.minimum((t + 1) * (SCAN_STEPS // 2), last_pair), 0)

    return (pl.BlockSpec((SEQ_GROUP, channels), prev),
            pl.BlockSpec((2 * SEQ_GROUP, channels), nxt))


def _conv_centred(x, prev, nxt, k, n, w_ref, b_ref):
    prev = jnp.where(k > 0, prev, 0.0)
    nxt = jnp.where(k < n - 1, nxt, 0.0)
    xp = jnp.concatenate([prev, x, nxt], axis=0)
    r = x.shape[0]
    out = b_ref[...] + xp[0:r] * w_ref[0:1, :]
    for j in range(1, CONV_K):
        out = out + xp[j * SEQ_GROUP:j * SEQ_GROUP + r] * w_ref[j:j + 1, :]
    return out


def _lru_scan_kernel(x_ref, xprev_ref, xnext_ref, h0_ref, cw_ref, cb_ref, wr_ref, wi_ref,
                     br_ref, bi_ref, lam_ref, h_ref, fin_ref, st_ref, a_ref, b_ref,
                     *, lay, reverse):
    _, _, first, last, k, n = _schedule(lay, SCAN_ROWS, reverse, pl.program_id(0))

    @pl.when(first)
    def _():
        st_ref[...] = h0_ref[0]

    x = _conv_centred(x_ref[...], xprev_ref[...], xnext_ref[...], k, n, cw_ref, cb_ref)
    xb = x.astype(BF16)
    bs = wr_ref.shape[-1]
    neg_sp = -LRU_C * _softplus(-lam_ref[...])
    for blk in range(LRU_BLOCKS):
        cols = slice(blk * bs, (blk + 1) * bs)
        r = jax.nn.sigmoid(_dot(xb[:, cols], wr_ref[blk]) + br_ref[:, cols])
        gi = jax.nn.sigmoid(_dot(xb[:, cols], wi_ref[blk]) + bi_ref[:, cols])
        log_a = neg_sp[:, cols] * r
        th = jnp.tanh(log_a)
        a_ref[:, cols] = jnp.exp(log_a)
        b_ref[:, cols] = jnp.sqrt(-2.0 * th / (1.0 - th)) * (gi * x[:, cols])

    steps = range(SCAN_STEPS - 1, -1, -1) if reverse else range(SCAN_STEPS)
    h = st_ref[...]
    for t in steps:
        rows = slice(t * SEQ_GROUP, (t + 1) * SEQ_GROUP)
        h = a_ref[rows, :] * h + b_ref[rows, :]
        h_ref[rows, :] = h
    st_ref[...] = h

    @pl.when(last)
    def _():
        fin_ref[0] = st_ref[...]


def _lru_scan(lay, xb, h0, conv_w, conv_b, wr, wi, br, bi, lam, reverse):
    rows, d = xb.shape
    sched = functools.partial(_schedule, lay, SCAN_ROWS, reverse)
    tile = lambda i: (sched(i)[0], 0)
    grp = lambda i: (sched(i)[1], 0, 0)
    const2 = lambda a: pl.BlockSpec(a.shape, lambda i: (0, 0))
    const3 = lambda a: pl.BlockSpec(a.shape, lambda i: (0, 0, 0))
    prev_spec, next_spec = _halo_specs(lay, d, reverse)
    return pl.pallas_call(
        functools.partial(_lru_scan_kernel, lay=lay, reverse=reverse),
        grid=(lay.tiles(SCAN_ROWS),),
        in_specs=[pl.BlockSpec((SCAN_ROWS, d), tile), prev_spec, next_spec,
                  pl.BlockSpec((1, SEQ_GROUP, d), grp),
                  const2(conv_w), const2(conv_b), const3(wr), const3(wi),
                  const2(br), const2(bi), const2(lam)],
        out_specs=[pl.BlockSpec((SCAN_ROWS, d), tile), pl.BlockSpec((1, SEQ_GROUP, d), grp)],
        out_shape=[jax.ShapeDtypeStruct((rows, d), F32), jax.ShapeDtypeStruct(h0.shape, F32)],
        scratch_shapes=[pltpu.VMEM((SEQ_GROUP, d), F32),
                        pltpu.VMEM((SCAN_ROWS, d), F32), pltpu.VMEM((SCAN_ROWS, d), F32)],
        compiler_params=_params(),
    )(xb, xb, xb, h0, conv_w, conv_b, wr, wi, br, bi, lam)


def _gdn_prep_kernel(x_ref, xprev_ref, xnext_ref, ab_ref, cw_ref, cb_ref, alog_ref, dtb_ref,
                     qkv_ref, gate_ref, *, lay):
    _, _, _, _, k, n = _schedule(lay, SCAN_ROWS, False, pl.program_id(0))
    x = _conv_centred(x_ref[...], xprev_ref[...], xnext_ref[...], k, n, cw_ref, cb_ref)
    x = jax.nn.silu(x)
    db = x.shape[1] // 3
    dk = db // GDN_HEADS
    for h in range(2 * GDN_HEADS):
        sl = slice(h * dk, (h + 1) * dk)
        xh = x[:, sl]
        xh = xh * lax.rsqrt(jnp.sum(xh * xh, axis=-1, keepdims=True) + EPS)
        if h < GDN_HEADS:
            xh = xh * (dk ** -0.5)
        qkv_ref[:, sl] = xh
    qkv_ref[:, 2 * db:] = x[:, 2 * db:]
    ab = ab_ref[...]
    g = -jnp.exp(alog_ref[...]) * _softplus(ab + dtb_ref[...])
    beta = jax.nn.sigmoid(ab)
    lane = lax.broadcasted_iota(jnp.int32, ab.shape, 1)
    gate_ref[...] = jnp.where(lane < 2 * GDN_HEADS, g, beta)


def _gdn_prep(lay, qkv, ab, conv_w, conv_b, alog, dtb):
    rows, c = qkv.shape
    tile = lambda i: (i, 0)
    const2 = lambda a: pl.BlockSpec(a.shape, lambda i: (0, 0))
    prev_spec, next_spec = _halo_specs(lay, c)
    return pl.pallas_call(
        functools.partial(_gdn_prep_kernel, lay=lay),
        grid=(lay.tiles(SCAN_ROWS),),
        in_specs=[pl.BlockSpec((SCAN_ROWS, c), tile), prev_spec, next_spec,
                  pl.BlockSpec((SCAN_ROWS, LANES), tile),
                  const2(conv_w), const2(conv_b), const2(alog), const2(dtb)],
        out_specs=[pl.BlockSpec((SCAN_ROWS, c), tile), pl.BlockSpec((SCAN_ROWS, LANES), tile)],
        out_shape=[jax.ShapeDtypeStruct((rows, c), F32), jax.ShapeDtypeStruct((rows, LANES), F32)],
        compiler_params=_params(),
    )(qkv, qkv, qkv, ab, conv_w, conv_b, alog, dtb)


def _gdn_kernel(qkv_ref, gate_ref, s0_ref, o_ref, fin_ref, s_ref, *, lay, reverse, direction):
    _, _, first, last, _, _ = _schedule(lay, GDN_ROWS, reverse, pl.program_id(0))
    c = GDN_CHUNK
    db = qkv_ref.shape[1] // 3
    dk = db // GDN_HEADS

    @pl.when(first)
    def _():
        s_ref[...] = s0_ref[0]

    ri = lax.broadcasted_iota(jnp.int32, (c, c), 0)
    ci = lax.broadcasted_iota(jnp.int32, (c, c), 1)
    incl = (ri <= ci) if reverse else (ri >= ci)
    strict = (ri < ci) if reverse else (ri > ci)
    tri = jnp.where(incl, 1.0, 0.0).astype(F32)
    eye = jnp.where(ri == ci, 1.0, 0.0).astype(F32)
    n_chunks = GDN_ROWS // c
    order = range(n_chunks - 1, -1, -1) if reverse else range(n_chunks)
    end_row = 0 if reverse else c - 1
    for ch in order:
        rows = slice(ch * c, (ch + 1) * c)
        gates = gate_ref[rows, :]
        gc = _dot_hi(tri, gates)
        gc_t = gc.T
        for h in range(GDN_HEADS):
            lane = direction * GDN_HEADS + h
            q = qkv_ref[rows, h * dk:(h + 1) * dk]
            k = qkv_ref[rows, db + h * dk:db + (h + 1) * dk]
            v = qkv_ref[rows, 2 * db + h * dk:2 * db + (h + 1) * dk]
            beta = gates[:, 2 * GDN_HEADS + lane:2 * GDN_HEADS + lane + 1]
            gcol = gc[:, lane:lane + 1]
            grow = gc_t[lane:lane + 1, :]
            decay = jnp.exp(jnp.where(incl, gcol - grow, -1e30))
            kb = k * beta
            vb = v * beta
            lmat = jnp.where(strict, _dot_nt_hi(kb, k) * decay, 0.0)
            t_pow = -lmat
            inv = eye + t_pow
            m = 1
            while 2 * m < c:
                t_pow = _dot_hi(t_pow, t_pow)
                inv = inv + _dot_hi(t_pow, inv)
                m *= 2
            egc = jnp.exp(gcol)
            s = s_ref[h]
            sb = s.astype(BF16)
            rhs = vb - _dot((kb * egc).astype(BF16), sb)
            v_new = _dot_hi(inv, rhs)
            qk = jnp.where(incl, _dot_nt_hi(q, k) * decay, 0.0)
            o = _dot((q * egc).astype(BF16), sb) + _dot(qk.astype(BF16), v_new.astype(BF16))
            o_ref[rows, h * dk:(h + 1) * dk] = o
            g_end = gcol[end_row:end_row + 1, :]
            kd = k * jnp.exp(g_end - gcol)
            s_ref[h] = s * jnp.exp(g_end) + _dot_tn(kd.astype(BF16), v_new.astype(BF16))

    @pl.when(last)
    def _():
        fin_ref[0] = s_ref[...]


def _gdn_scan(lay, qkv, gates, s0, reverse, direction):
    rows, c3 = qkv.shape
    db = c3 // 3
    sched = functools.partial(_schedule, lay, GDN_ROWS, reverse)
    tile = lambda i: (sched(i)[0], 0)
    seq = lambda i: (sched(i)[1], 0, 0, 0)
    sblock = (1,) + s0.shape[1:]
    return pl.pallas_call(
        functools.partial(_gdn_kernel, lay=lay, reverse=reverse, direction=direction),
        grid=(lay.tiles(GDN_ROWS),),
        in_specs=[pl.BlockSpec((GDN_ROWS, c3), tile),
                  pl.BlockSpec((GDN_ROWS, LANES), tile),
                  pl.BlockSpec(sblock, seq)],
        out_specs=[pl.BlockSpec((GDN_ROWS, db), tile), pl.BlockSpec(sblock, seq)],
        out_shape=[jax.ShapeDtypeStruct((rows, db), F32), jax.ShapeDtypeStruct(s0.shape, F32)],
        scratch_shapes=[pltpu.VMEM(s0.shape[1:], F32)],
        compiler_params=_params(),
    )(qkv, gates, s0)


def _to_time_major(x):
    b, t, c = x.shape
    return x.reshape(b // SEQ_GROUP, SEQ_GROUP, t, c).transpose(0, 2, 1, 3).reshape(b * t, c)


def _from_time_major(x, b, t):
    c = x.shape[-1]
    return x.reshape(b // SEQ_GROUP, t, SEQ_GROUP, c).transpose(0, 2, 1, 3).reshape(b, t, c)


def _swap_major(lay_from, x, to_seq):
    c = x.shape[-1]
    parts = []
    for off, n, length in ((0, lay_from.n_ctx, lay_from.len_ctx),
                           (lay_from.ctx_rows, lay_from.n_lat, lay_from.len_lat)):
        t = length // SEQ_GROUP
        p = x[off:off + n * length]
        if to_seq:
            p = p.reshape(n, t, SEQ_GROUP, c).transpose(0, 2, 1, 3)
        else:
            p = p.reshape(n, SEQ_GROUP, t, c).transpose(0, 2, 1, 3)
        parts.append(p.reshape(n * length, c))
    return jnp.concatenate(parts, axis=0)


def _group_states(lat_state, n_ctx_groups):
    db = lat_state.shape[0]
    flat = lat_state.reshape(db // SEQ_GROUP, SEQ_GROUP, -1).astype(F32)
    zeros = jnp.zeros((n_ctx_groups,) + flat.shape[1:], F32)
    return jnp.concatenate([zeros, flat], axis=0)


def kernel(x_prompt, x_sample, state_s5_re, state_s5_im, state_delta, state_lru, c, c_ctx, w_ada, b_ada, norm_mix_pre, norm_mix_post, norm_mlp_pre, norm_mlp_post, w_mlp_in, w_mlp_out, w_in_even, w_out_even, s5_lam_re, s5_lam_im, s5_log_dt, s5_b_re, s5_b_im, s5_c_re, s5_c_im, s5_d, gdn_conv_w, gdn_conv_b, gdn_a_log, gdn_dt_bias, gdn_o_norm, w_in_odd, w_out_odd, lru_conv_w, lru_conv_b, lru_w_r, lru_b_r, lru_w_i, lru_b_i, lru_lam):
    bp, tp, d = x_prompt.shape
    bl, tl, _ = x_sample.shape
    depth = w_ada.shape[0]
    n_dir = 2
    da = s5_d.shape[1]
    db = gdn_conv_w.shape[2] // 3
    heads = GDN_HEADS
    ngrp = s5_lam_re.shape[2]

    lay = Layout(bp // SEQ_GROUP, tp * SEQ_GROUP, bl // SEQ_GROUP, tl * SEQ_GROUP)
    lay_seq = Layout(bp, tp, bl, tl)

    cond = jnp.concatenate([c_ctx[None].astype(F32), c.astype(F32)], axis=0)
    pad = (-cond.shape[0]) % SUBLANES
    cond = jnp.pad(cond, ((0, pad), (0, 0)))
    ada = _ada_vectors(cond, w_ada, b_ada)
    mod_ctx = jnp.broadcast_to(ada[:, 0:1], (depth, SEQ_GROUP, ada.shape[-1]))[:, None]
    mod_lat = ada[:, 1:1 + bl].reshape(depth, bl // SEQ_GROUP, SEQ_GROUP, -1)
    mod = jnp.concatenate([mod_ctx, mod_lat], axis=1)

    x_lat = _add_positions(x_sample, _grid_sincos_table(tl, d))
    x = jnp.concatenate([_to_time_major(x_prompt), _to_time_major(x_lat)], axis=0)

    a_re, a_im, bb_re, bb_im = _s5_discretise(s5_lam_re, s5_lam_im, s5_log_dt, s5_b_re, s5_b_im)
    c_re_t = s5_c_re
    c_im_t = s5_c_im

    row = lambda v: v.reshape(1, -1).astype(F32)
    new_re, new_im, new_delta, new_lru = [], [], [], []
    for l in range(depth):
        mod_l = mod[l]
        if l % 2 == 0:
            e = l // 2
            n_in = w_in_even.shape[2]
            n_pad = (-n_in) % LANES
            w_in = jnp.pad(w_in_even[e], ((0, 0), (0, n_pad))).astype(BF16)
            splits = ((0, da), (da, 2 * da), (2 * da, 2 * da + 3 * db),
                      (2 * da + 3 * db, 2 * da + 4 * db), (2 * da + 4 * db, n_in + n_pad))
            u, za, qkv, zb, ab = _pre_mixer(lay, x, mod_l, row(norm_mix_pre[l]), w_in, splits)

            cw = jnp.stack([_block_diag_out(c_re_t[e]), _block_diag_out(c_im_t[e])]).astype(BF16)
            ys, fr, fi = [], [], []
            for dd in range(n_dir):
                a8 = jnp.stack([a_re[e, dd].reshape(-1), a_im[e, dd].reshape(-1)])
                a8 = jnp.broadcast_to(a8[:, None, :], (2, SEQ_GROUP, a8.shape[-1]))
                bw = jnp.concatenate([_block_diag_in(bb_re[e, dd]), _block_diag_in(bb_im[e, dd])],
                                     axis=-1).astype(BF16)
                y, f_re, f_im = _s5_scan(lay, u, _group_states(state_s5_re[:, e, dd], lay.n_ctx),
                                         _group_states(state_s5_im[:, e, dd], lay.n_ctx),
                                         a8, bw, cw, reverse=(dd == 1))
                ys.append(y)
                fr.append(f_re[:lay.n_ctx].reshape(bp, ngrp, -1))
                fi.append(f_im[:lay.n_ctx].reshape(bp, ngrp, -1))
            new_re.append(jnp.stack(fr, axis=1))
            new_im.append(jnp.stack(fi, axis=1))

            pad8 = lambda v: jnp.pad(v.reshape(1, -1).astype(F32), ((0, 0), (0, LANES - v.size)))
            qkv_n, gates = _gdn_prep(lay, qkv, ab, gdn_conv_w[e].astype(F32), row(gdn_conv_b[e]),
                                     pad8(gdn_a_log[e]), pad8(gdn_dt_bias[e]))
            qkv_s = _swap_major(lay, qkv_n, to_seq=True)
            gates_s = _swap_major(lay, gates, to_seq=True)
            os_, fd = [], []
            for dd in range(n_dir):
                s0 = jnp.concatenate([jnp.zeros((bp,) + state_delta.shape[3:], F32),
                                      state_delta[:, e, dd].astype(F32)], axis=0)
                o_s, s_fin = _gdn_scan(lay_seq, qkv_s, gates_s, s0, reverse=(dd == 1), direction=dd)
                os_.append(_swap_major(lay, o_s, to_seq=False))
                fd.append(s_fin[:bp])
            new_delta.append(jnp.stack(fd, axis=1))

            x = _post_even(lay, u, za, ys[0], ys[1], os_[0], os_[1], zb, x, mod_l,
                           row(s5_d[e]), row(gdn_o_norm[e]), row(norm_mix_post[l]),
                           w_out_even[e].astype(BF16))
        else:
            o = l // 2
            dr = lru_lam.shape[2]
            w_in = w_in_odd[o].astype(BF16)
            xb, yg = _pre_mixer(lay, x, mod_l, row(norm_mix_pre[l]), w_in, ((0, dr), (dr, 2 * dr)))
            hs, fl = [], []
            for dd in range(n_dir):
                h, h_fin = _lru_scan(lay, xb, _group_states(state_lru[:, o, dd], lay.n_ctx),
                                     lru_conv_w[o].astype(F32), row(lru_conv_b[o]),
                                     lru_w_r[o, dd].astype(BF16), lru_w_i[o, dd].astype(BF16),
                                     row(lru_b_r[o, dd]), row(lru_b_i[o, dd]), row(lru_lam[o, dd]),
                                     reverse=(dd == 1))
                hs.append(h)
                fl.append(h_fin[:lay.n_ctx].reshape(bp, dr))
            new_lru.append(jnp.stack(fl, axis=1))
            x = _post_odd(lay, hs[0], hs[1], yg, x, mod_l, row(norm_mix_post[l]),
                          w_out_odd[o].astype(BF16))

        x = _mlp(lay, x, mod_l, row(norm_mlp_pre[l]), row(norm_mlp_post[l]),
                 w_mlp_in[l].astype(BF16), w_mlp_out[l].astype(BF16))

    y_prompt = _from_time_major(x[:lay.ctx_rows], bp, tp).astype(x_prompt.dtype)
    y_sample = _from_time_major(x[lay.ctx_rows:], bl, tl).astype(x_sample.dtype)
    p_a = s5_lam_re.shape[3]
    new_s5_re = jnp.stack(new_re, axis=1).reshape(bp, -1, n_dir, ngrp, p_a)
    new_s5_im = jnp.stack(new_im, axis=1).reshape(bp, -1, n_dir, ngrp, p_a)
    return (y_prompt, y_sample, new_s5_re, new_s5_im,
            jnp.stack(new_delta, axis=1), jnp.stack(new_lru, axis=1))
```

```python
import functools
from typing import NamedTuple

import jax
import jax.numpy as jnp
from jax import lax
from jax.experimental import pallas as pl
from jax.experimental.pallas import tpu as pltpu

F32 = jnp.float32
BF16 = jnp.bfloat16
HIGHEST = lax.Precision.HIGHEST

EPS = 1e-6
POS_BASE = 10000.0
GRID_W = 64
N_MOD = 6
S5_GROUP = 16
S5_STATES = 64
GDN_HEADS = 4
GDN_CHUNK = 64
CONV_K = 4
CONV_LEFT = (CONV_K - 1) // 2
LRU_BLOCKS = 4
LRU_C = 8.0

SUBLANES = 8
LANES = 128
SEQ_GROUP = SUBLANES
SCAN_STEPS = 32
SCAN_ROWS = SCAN_STEPS * SEQ_GROUP
GDN_TILE_ROWS = GDN_CHUNK * SEQ_GROUP
TOKEN_ROWS = 512
S5_SLICE = 128
VMEM_LIMIT_BYTES = 56 * 1024 * 1024


class Layout(NamedTuple):
    n_ctx: int
    len_ctx: int
    n_lat: int
    len_lat: int

    @property
    def ctx_rows(self):
        return self.n_ctx * self.len_ctx

    @property
    def rows(self):
        return self.ctx_rows + self.n_lat * self.len_lat

    @property
    def groups(self):
        return self.n_ctx + self.n_lat

    def tiles(self, tile_rows):
        return self.ctx_rows // tile_rows + self.n_lat * (self.len_lat // tile_rows)


def _schedule(lay, tile_rows, reverse, i):
    nc = lay.len_ctx // tile_rows
    nl = lay.len_lat // tile_rows
    n_ctx_tiles = lay.n_ctx * nc
    is_ctx = i < n_ctx_tiles
    j = jnp.maximum(i - n_ctx_tiles, 0)
    ic = jnp.minimum(i, n_ctx_tiles - 1)
    g = jnp.where(is_ctx, ic // nc, lay.n_ctx + j // nl)
    step = jnp.where(is_ctx, ic % nc, j % nl)
    n = jnp.where(is_ctx, nc, nl)
    k = (n - 1 - step) if reverse else step
    base = jnp.where(is_ctx, (ic // nc) * nc, n_ctx_tiles + (j // nl) * nl)
    return base + k, g, step == 0, step == n - 1, k, n


def _params(n_axes=1):
    return pltpu.CompilerParams(dimension_semantics=("arbitrary",) * n_axes,
                                vmem_limit_bytes=VMEM_LIMIT_BYTES)


def _dot(a, b):
    return jnp.dot(a, b, preferred_element_type=F32)


def _dot_hi(a, b):
    return jnp.dot(a, b, precision=HIGHEST, preferred_element_type=F32)


def _dot_nt(a, b):
    return lax.dot_general(a, b, (((1,), (1,)), ((), ())), preferred_element_type=F32)


def _dot_tn(a, b):
    return lax.dot_general(a, b, (((0,), (0,)), ((), ())), preferred_element_type=F32)


def _rms(x):
    return x * lax.rsqrt(jnp.mean(x * x, axis=-1, keepdims=True) + EPS)


def _per_group(rows_val, vec8):
    r, c = rows_val.shape
    return rows_val.reshape(r // SEQ_GROUP, SEQ_GROUP, c), vec8[None]


def _modulate(h, shift8, scale8):
    h3, sc = _per_group(h, scale8)
    out = h3 * (1.0 + sc) + shift8[None]
    return out.reshape(h.shape)


def _gated_residual(x, y, gate8):
    y3, g = _per_group(y, gate8)
    return x + (y3 * g).reshape(x.shape)


def _mod_chunk(mod_ref, idx, d):
    return mod_ref[0, :, idx * d:(idx + 1) * d]


def _softplus(x):
    return jnp.maximum(x, 0.0) + jnp.log1p(jnp.exp(-jnp.abs(x)))


def _ada_kernel(c_ref, w_ref, b_ref, o_ref):
    s = jax.nn.silu(c_ref[...]).astype(BF16)
    o_ref[0] = _dot(s, w_ref[0].astype(BF16)) + b_ref[0]


def _ada_vectors(cond, w_ada, b_ada):
    depth, d, n = w_ada.shape
    rows = cond.shape[0]
    tn = 512
    return pl.pallas_call(
        _ada_kernel, name="ada_vectors",
        grid=(depth, n // tn),
        in_specs=[pl.BlockSpec((rows, d), lambda l, j: (0, 0)),
                  pl.BlockSpec((1, d, tn), lambda l, j: (l, 0, j)),
                  pl.BlockSpec((1, 1, tn), lambda l, j: (l, 0, j))],
        out_specs=pl.BlockSpec((1, rows, tn), lambda l, j: (l, 0, j)),
        out_shape=jax.ShapeDtypeStruct((depth, rows, n), F32),
        compiler_params=_params(2),
    )(cond, w_ada, b_ada.reshape(depth, 1, n))


def _pos_kernel(x_ref, p_ref, o_ref):
    o_ref[0] = x_ref[0] + p_ref[...]


def _grid_sincos_table(n_tokens, d_model):
    rows = n_tokens // GRID_W
    row = jnp.repeat(jnp.arange(rows, dtype=F32), GRID_W)
    col = jnp.tile(jnp.arange(GRID_W, dtype=F32), rows)
    n_freq = d_model // 4
    omega = POS_BASE ** (-jnp.arange(n_freq, dtype=F32) / n_freq)
    ar = row[:, None] * omega
    ac = col[:, None] * omega
    return jnp.concatenate([jnp.sin(ar), jnp.cos(ar), jnp.sin(ac), jnp.cos(ac)], axis=-1)


def _add_positions(x, table):
    b, t, d = x.shape
    tt = min(t, 512)
    return pl.pallas_call(
        _pos_kernel, name="add_positions",
        grid=(b, t // tt),
        in_specs=[pl.BlockSpec((1, tt, d), lambda i, j: (i, j, 0)),
                  pl.BlockSpec((tt, d), lambda i, j: (j, 0))],
        out_specs=pl.BlockSpec((1, tt, d), lambda i, j: (i, j, 0)),
        out_shape=jax.ShapeDtypeStruct(x.shape, x.dtype),
        compiler_params=_params(2),
    )(x, table)


def _mod_index(lay, tile_rows):
    def index(i):
        row = i * tile_rows
        lat = jnp.maximum(row - lay.ctx_rows, 0) // lay.len_lat
        return jnp.where(row < lay.ctx_rows, 0, 1 + lat)
    return index


def _pre_kernel(x_ref, mod_ref, gain_ref, w_ref, *out_refs, splits):
    d = x_ref.shape[-1]
    h = _rms(x_ref[...]) * gain_ref[...]
    h = _modulate(h, _mod_chunk(mod_ref, 0, d), _mod_chunk(mod_ref, 1, d)).astype(BF16)
    for (a, b, slabs), o_ref in zip(splits, out_refs):
        if slabs:
            for j in range((b - a) // LANES):
                o_ref[j] = _dot(h, w_ref[:, a + j * LANES:a + (j + 1) * LANES])
        else:
            o_ref[...] = _dot(h, w_ref[:, a:b])


def _pre_mixer(lay, x, mod, gain, w, splits):
    rows, d = x.shape
    n = w.shape[1]
    midx = _mod_index(lay, TOKEN_ROWS)
    out_specs, out_shapes = [], []
    for a, b, slabs in splits:
        if slabs:
            ns = (b - a) // LANES
            out_specs.append(pl.BlockSpec((ns, TOKEN_ROWS, LANES), lambda i: (0, i, 0)))
            out_shapes.append(jax.ShapeDtypeStruct((ns, rows, LANES), F32))
        else:
            out_specs.append(pl.BlockSpec((TOKEN_ROWS, b - a), lambda i: (i, 0)))
            out_shapes.append(jax.ShapeDtypeStruct((rows, b - a), F32))
    return pl.pallas_call(
        functools.partial(_pre_kernel, splits=splits), name="pre_mixer",
        grid=(rows // TOKEN_ROWS,),
        in_specs=[pl.BlockSpec((TOKEN_ROWS, d), lambda i: (i, 0)),
                  pl.BlockSpec((1, SEQ_GROUP, N_MOD * d), lambda i: (midx(i), 0, 0)),
                  pl.BlockSpec((1, d), lambda i: (0, 0)),
                  pl.BlockSpec((d, n), lambda i: (0, 0))],
        out_specs=out_specs, out_shape=out_shapes,
        compiler_params=_params(),
    )(x, mod, gain, w)


def _finish_sublayer(x, out, gain, gate8):
    return _gated_residual(x, _rms(out) * gain, gate8)


def _post_odd_kernel(hf_ref, hb_ref, yg_ref, x_ref, mod_ref, gain_ref, w_ref, o_ref):
    d = x_ref.shape[-1]
    y = (hf_ref[...] + hb_ref[...]) * jax.nn.gelu(yg_ref[...])
    out = _dot(y.astype(BF16), w_ref[...])
    o_ref[...] = _finish_sublayer(x_ref[...], out, gain_ref[...], _mod_chunk(mod_ref, 2, d))


def _post_odd(lay, hf, hb, yg, x, mod, gain, w):
    rows, d = x.shape
    dr = hf.shape[1]
    midx = _mod_index(lay, TOKEN_ROWS)
    row_spec = lambda c: pl.BlockSpec((TOKEN_ROWS, c), lambda i: (i, 0))
    return pl.pallas_call(
        _post_odd_kernel, name="post_odd",
        grid=(rows // TOKEN_ROWS,),
        in_specs=[row_spec(dr), row_spec(dr), row_spec(dr), row_spec(d),
                  pl.BlockSpec((1, SEQ_GROUP, N_MOD * d), lambda i: (midx(i), 0, 0)),
                  pl.BlockSpec((1, d), lambda i: (0, 0)),
                  pl.BlockSpec((dr, d), lambda i: (0, 0))],
        out_specs=row_spec(d),
        out_shape=jax.ShapeDtypeStruct((rows, d), F32),
        compiler_params=_params(),
    )(hf, hb, yg, x, mod, gain, w)


def _post_even_kernel(u_ref, za_ref, yf_ref, yb_ref, of_ref, ob_ref, zb_ref, x_ref, mod_ref,
                      dskip_ref, onorm_ref, gain_ref, w_ref, o_ref):
    d = x_ref.shape[-1]
    da = u_ref.shape[-1]
    heads, _, dv = of_ref.shape
    ya = jax.nn.gelu(yf_ref[...] + yb_ref[...] + u_ref[...] * dskip_ref[...])
    ya = ya * jax.nn.sigmoid(za_ref[...])
    out = _dot(ya.astype(BF16), w_ref[0:da, :])
    for h in range(heads):
        o = of_ref[h] + ob_ref[h]
        oh = _rms(o) * onorm_ref[...] * jax.nn.silu(zb_ref[:, h * dv:(h + 1) * dv])
        out = out + _dot(oh.astype(BF16), w_ref[da + h * dv:da + (h + 1) * dv, :])
    o_ref[...] = _finish_sublayer(x_ref[...], out, gain_ref[...], _mod_chunk(mod_ref, 2, d))


def _post_even(lay, u, za, yf, yb, of, ob, zb, x, mod, dskip, onorm, gain, w):
    rows, d = x.shape
    da, db = u.shape[1], zb.shape[1]
    midx = _mod_index(lay, TOKEN_ROWS)
    row_spec = lambda c: pl.BlockSpec((TOKEN_ROWS, c), lambda i: (i, 0))
    slab_spec = pl.BlockSpec((of.shape[0], TOKEN_ROWS, of.shape[2]), lambda i: (0, i, 0))
    const = lambda shape: pl.BlockSpec(shape, lambda i: (0, 0))
    return pl.pallas_call(
        _post_even_kernel, name="post_even",
        grid=(rows // TOKEN_ROWS,),
        in_specs=[row_spec(da), row_spec(da), row_spec(da), row_spec(da),
                  slab_spec, slab_spec, row_spec(db), row_spec(d),
                  pl.BlockSpec((1, SEQ_GROUP, N_MOD * d), lambda i: (midx(i), 0, 0)),
                  const((1, da)), const((1, db // GDN_HEADS)), const((1, d)), const((da + db, d))],
        out_specs=row_spec(d),
        out_shape=jax.ShapeDtypeStruct((rows, d), F32),
        compiler_params=_params(),
    )(u, za, yf, yb, of, ob, zb, x, mod, dskip, onorm, gain, w)


def _mlp_kernel(x_ref, mod_ref, gpre_ref, gpost_ref, w1_ref, w2_ref, o_ref):
    d = x_ref.shape[-1]
    x = x_ref[...]
    h = _rms(x) * gpre_ref[...]
    h = _modulate(h, _mod_chunk(mod_ref, 3, d), _mod_chunk(mod_ref, 4, d)).astype(BF16)
    f = jnp.square(jnp.maximum(_dot(h, w1_ref[...]), 0.0))
    out = _dot(f.astype(BF16), w2_ref[...])
    o_ref[...] = _finish_sublayer(x, out, gpost_ref[...], _mod_chunk(mod_ref, 5, d))


def _mlp(lay, x, mod, gpre, gpost, w1, w2):
    rows, d = x.shape
    dff = w1.shape[1]
    midx = _mod_index(lay, TOKEN_ROWS)
    resident = lambda shape: pl.BlockSpec(shape, lambda i: (0, 0), pipeline_mode=pl.Buffered(1))
    return pl.pallas_call(
        _mlp_kernel, name="mlp",
        grid=(rows // TOKEN_ROWS,),
        in_specs=[pl.BlockSpec((TOKEN_ROWS, d), lambda i: (i, 0)),
                  pl.BlockSpec((1, SEQ_GROUP, N_MOD * d), lambda i: (midx(i), 0, 0)),
                  pl.BlockSpec((1, d), lambda i: (0, 0)),
                  pl.BlockSpec((1, d), lambda i: (0, 0)),
                  resident((d, dff)), resident((dff, d))],
        out_specs=pl.BlockSpec((TOKEN_ROWS, d), lambda i: (i, 0)),
        out_shape=jax.ShapeDtypeStruct((rows, d), F32),
        compiler_params=_params(),
    )(x, mod, gpre, gpost, w1, w2)


def _s5_discretise_kernel(lr_ref, li_ref, ldt_ref, bre_ref, bim_ref,
                          ar_ref, ai_ref, bbre_ref, bbim_ref):
    lr, li = lr_ref[...], li_ref[...]
    dt = jnp.exp(ldt_ref[...])
    mag = jnp.exp(lr * dt)
    ar = mag * jnp.cos(li * dt)
    ai = mag * jnp.sin(li * dt)
    den = lr * lr + li * li
    fr = ((ar - 1.0) * lr + ai * li) / den
    fi = (ai * lr - (ar - 1.0) * li) / den
    ar_ref[...] = ar
    ai_ref[...] = ai
    bbre_ref[...] = fr * bre_ref[...] - fi * bim_ref[...]
    bbim_ref[...] = fr * bim_ref[...] + fi * bre_ref[...]


def _s5_discretise(lam_re, lam_im, log_dt, b_re, b_im):
    e, nd, g, p = lam_re.shape
    c = b_re.shape[-1]
    full = (e, nd, g, c, p)
    flat = lambda a: jnp.broadcast_to(a, full).reshape(-1, p)
    lr = flat(lam_re[:, :, :, None, :])
    li = flat(lam_im[:, :, :, None, :])
    ldt = flat(log_dt[:, :, :, None, None])
    bre = flat(b_re.transpose(0, 1, 3, 2)[:, None])
    bim = flat(b_im.transpose(0, 1, 3, 2)[:, None])
    shape = jax.ShapeDtypeStruct(lr.shape, F32)
    ar, ai, bbre, bbim = pl.pallas_call(
        _s5_discretise_kernel, name="s5_discretise", out_shape=[shape] * 4,
        compiler_params=pltpu.CompilerParams(vmem_limit_bytes=VMEM_LIMIT_BYTES),
    )(lr, li, ldt, bre, bim)
    ar = ar.reshape(full)[:, :, :, 0, :]
    ai = ai.reshape(full)[:, :, :, 0, :]
    return ar, ai, bbre.reshape(full), bbim.reshape(full)


def _block_diag_in(bb):
    g, c, p = bb.shape
    per = S5_SLICE // c
    n = g // per
    blocks = bb.reshape(n, per, c, p)
    eye = jnp.eye(per, dtype=bb.dtype)
    return jnp.einsum('nicp,ij->nicjp', blocks, eye).reshape(n, per * c, per * p)


def _block_diag_out(cc):
    g, c, p = cc.shape
    per = S5_SLICE // c
    n = g // per
    blocks = cc.reshape(n, per, c, p)
    eye = jnp.eye(per, dtype=cc.dtype)
    return jnp.einsum('nicp,ij->nipjc', blocks, eye).reshape(n, per * p, per * c)


def _s5_scan_kernel(u_ref, h0re_ref, h0im_ref, a_ref, bw_ref, cw_ref,
                    y_ref, fre_ref, fim_ref, hre_ref, him_ref, bre_ref, bim_ref,
                    *, lay, reverse):
    _, _, first, last, _, _ = _schedule(lay, SCAN_ROWS, reverse, pl.program_id(0))
    n_slices = bw_ref.shape[0]
    wide = bw_ref.shape[2] // 2

    @pl.when(first)
    def _():
        hre_ref[...] = h0re_ref[0]
        him_ref[...] = h0im_ref[0]

    for j in range(n_slices):
        uj = u_ref[:, j * S5_SLICE:(j + 1) * S5_SLICE].astype(BF16)
        bb = _dot(uj, bw_ref[j])
        bre_ref[:, j * wide:(j + 1) * wide] = bb[:, :wide]
        bim_ref[:, j * wide:(j + 1) * wide] = bb[:, wide:]

    steps = range(SCAN_STEPS - 1, -1, -1) if reverse else range(SCAN_STEPS)
    for j in range(n_slices):
        cols = slice(j * wide, (j + 1) * wide)
        ar = a_ref[0, :, cols]
        ai = a_ref[1, :, cols]
        hr = hre_ref[:, cols]
        hi = him_ref[:, cols]
        for t in steps:
            rows = slice(t * SEQ_GROUP, (t + 1) * SEQ_GROUP)
            nr = ar * hr - ai * hi + bre_ref[rows, cols]
            ni = ar * hi + ai * hr + bim_ref[rows, cols]
            bre_ref[rows, cols] = nr
            bim_ref[rows, cols] = ni
            hr, hi = nr, ni
        hre_ref[:, cols] = hr
        him_ref[:, cols] = hi

    for j in range(n_slices):
        cols = slice(j * wide, (j + 1) * wide)
        yj = _dot(bre_ref[:, cols].astype(BF16), cw_ref[0, j])
        yj = yj - _dot(bim_ref[:, cols].astype(BF16), cw_ref[1, j])
        y_ref[:, j * S5_SLICE:(j + 1) * S5_SLICE] = yj

    @pl.when(last)
    def _():
        fre_ref[0] = hre_ref[...]
        fim_ref[0] = him_ref[...]


def _s5_scan(lay, u, h0re, h0im, a8, bw, cw, reverse):
    rows, da = u.shape
    ns = h0re.shape[-1]
    sched = functools.partial(_schedule, lay, SCAN_ROWS, reverse)
    tile = lambda i: (sched(i)[0], 0)
    grp = lambda i: (sched(i)[1], 0, 0)
    state_shape = jax.ShapeDtypeStruct(h0re.shape, F32)
    return pl.pallas_call(
        functools.partial(_s5_scan_kernel, lay=lay, reverse=reverse), name="s5_scan",
        grid=(lay.tiles(SCAN_ROWS),),
        in_specs=[pl.BlockSpec((SCAN_ROWS, da), tile),
                  pl.BlockSpec((1, SEQ_GROUP, ns), grp),
                  pl.BlockSpec((1, SEQ_GROUP, ns), grp),
                  pl.BlockSpec(a8.shape, lambda i: (0, 0, 0)),
                  pl.BlockSpec(bw.shape, lambda i: (0, 0, 0)),
                  pl.BlockSpec(cw.shape, lambda i: (0, 0, 0, 0))],
        out_specs=[pl.BlockSpec((SCAN_ROWS, da), tile),
                   pl.BlockSpec((1, SEQ_GROUP, ns), grp),
                   pl.BlockSpec((1, SEQ_GROUP, ns), grp)],
        out_shape=[jax.ShapeDtypeStruct((rows, da), F32), state_shape, state_shape],
        scratch_shapes=[pltpu.VMEM((SEQ_GROUP, ns), F32), pltpu.VMEM((SEQ_GROUP, ns), F32),
                        pltpu.VMEM((SCAN_ROWS, ns), F32), pltpu.VMEM((SCAN_ROWS, ns), F32)],
        compiler_params=_params(),
    )(u, h0re, h0im, a8, bw, cw)


def _halo_specs(lay, steps, channels, reverse=False, slabs=None):
    sched = functools.partial(_schedule, lay, steps * SEQ_GROUP, reverse)
    last_pair = lay.rows // (2 * SEQ_GROUP) - 1
    lead = () if slabs is None else (0,)
    shape = () if slabs is None else (slabs,)

    def prev(i):
        t = sched(i)[0]
        return lead + (jnp.maximum(t * steps - 1, 0), 0)

    def nxt(i):
        t = sched(i)[0]
        return lead + (jnp.minimum((t + 1) * (steps // 2), last_pair), 0)

    return (pl.BlockSpec(shape + (SEQ_GROUP, channels), prev),
            pl.BlockSpec(shape + (2 * SEQ_GROUP, channels), nxt))


def _conv_centred(x, prev, nxt, k, n, w, b):
    prev = jnp.where(k > 0, prev, 0.0)
    nxt = jnp.where(k < n - 1, nxt, 0.0)
    xp = jnp.concatenate([prev, x, nxt], axis=0)
    r = x.shape[0]
    out = b + xp[0:r] * w[0:1, :]
    for j in range(1, CONV_K):
        out = out + xp[j * SEQ_GROUP:j * SEQ_GROUP + r] * w[j:j + 1, :]
    return out


def _lru_scan_kernel(x_ref, xprev_ref, xnext_ref, h0_ref, cw_ref, cb_ref, wr_ref, wi_ref,
                     br_ref, bi_ref, lam_ref, h_ref, fin_ref, st_ref, a_ref, b_ref,
                     *, lay, reverse):
    _, _, first, last, k, n = _schedule(lay, SCAN_ROWS, reverse, pl.program_id(0))

    @pl.when(first)
    def _():
        st_ref[...] = h0_ref[0]

    x = _conv_centred(x_ref[...], xprev_ref[...], xnext_ref[...], k, n, cw_ref[...], cb_ref[...])
    xb = x.astype(BF16)
    bs = wr_ref.shape[-1]
    neg_sp = -LRU_C * _softplus(-lam_ref[...])
    for blk in range(LRU_BLOCKS):
        cols = slice(blk * bs, (blk + 1) * bs)
        r = jax.nn.sigmoid(_dot(xb[:, cols], wr_ref[blk]) + br_ref[:, cols])
        gi = jax.nn.sigmoid(_dot(xb[:, cols], wi_ref[blk]) + bi_ref[:, cols])
        log_a = neg_sp[:, cols] * r
        th = jnp.tanh(log_a)
        a_ref[:, cols] = jnp.exp(log_a)
        b_ref[:, cols] = jnp.sqrt(-2.0 * th / (1.0 - th)) * (gi * x[:, cols])

    steps = range(SCAN_STEPS - 1, -1, -1) if reverse else range(SCAN_STEPS)
    h = st_ref[...]
    for t in steps:
        rows = slice(t * SEQ_GROUP, (t + 1) * SEQ_GROUP)
        h = a_ref[rows, :] * h + b_ref[rows, :]
        h_ref[rows, :] = h
    st_ref[...] = h

    @pl.when(last)
    def _():
        fin_ref[0] = st_ref[...]


def _lru_scan(lay, xb, h0, conv_w, conv_b, wr, wi, br, bi, lam, reverse):
    rows, d = xb.shape
    sched = functools.partial(_schedule, lay, SCAN_ROWS, reverse)
    tile = lambda i: (sched(i)[0], 0)
    grp = lambda i: (sched(i)[1], 0, 0)
    const2 = lambda a: pl.BlockSpec(a.shape, lambda i: (0, 0))
    const3 = lambda a: pl.BlockSpec(a.shape, lambda i: (0, 0, 0))
    prev_spec, next_spec = _halo_specs(lay, SCAN_STEPS, d, reverse)
    return pl.pallas_call(
        functools.partial(_lru_scan_kernel, lay=lay, reverse=reverse), name="lru_scan",
        grid=(lay.tiles(SCAN_ROWS),),
        in_specs=[pl.BlockSpec((SCAN_ROWS, d), tile), prev_spec, next_spec,
                  pl.BlockSpec((1, SEQ_GROUP, d), grp),
                  const2(conv_w), const2(conv_b), const3(wr), const3(wi),
                  const2(br), const2(bi), const2(lam)],
        out_specs=[pl.BlockSpec((SCAN_ROWS, d), tile), pl.BlockSpec((1, SEQ_GROUP, d), grp)],
        out_shape=[jax.ShapeDtypeStruct((rows, d), F32), jax.ShapeDtypeStruct(h0.shape, F32)],
        scratch_shapes=[pltpu.VMEM((SEQ_GROUP, d), F32),
                        pltpu.VMEM((SCAN_ROWS, d), F32), pltpu.VMEM((SCAN_ROWS, d), F32)],
        compiler_params=_params(),
    )(xb, xb, xb, h0, conv_w, conv_b, wr, wi, br, bi, lam)


def _unit_triangular_inverses(lmats, eye, blocks):
    c = lmats[0].shape[0]
    ts = [jnp.where(blocks[0], -l, 0.0) for l in lmats]
    tbs = [t.astype(BF16) for t in ts]
    invs = [eye + t for t in ts]
    ps = [_dot(tb, tb) for tb in tbs]
    ys = [_dot(jnp.concatenate([x, p], axis=0).astype(BF16), p.astype(BF16))
          for x, p in zip(invs, ps)]
    invs = [x + y[:c] for x, y in zip(invs, ys)]
    zs = [_dot(x.astype(BF16), y[c:].astype(BF16)) for x, y in zip(invs, ys)]
    invs = [x + z for x, z in zip(invs, zs)]
    inside = blocks[0]
    for outer in blocks[1:] + [None]:
        off = ~inside if outer is None else (outer & ~inside)
        es = [jnp.where(off, l, 0.0).astype(BF16) for l in lmats]
        xbs = [x.astype(BF16) for x in invs]
        xes = [_dot(xb, e).astype(BF16) for xb, e in zip(xbs, es)]
        xexs = [_dot(xe, xb) for xe, xb in zip(xes, xbs)]
        invs = [x - xex for x, xex in zip(invs, xexs)]
        inside = outer
    return invs


def _gdn_kernel(x_ref, xprev_ref, xnext_ref, ab_ref, cw_ref, cb_ref, alog_ref, dtb_ref, s0_ref,
                o_ref, fin_ref, s_ref, qkv_ref, gate_ref, *, lay, reverse, direction):
    _, _, first, last, kpos, npos = _schedule(lay, GDN_TILE_ROWS, reverse, pl.program_id(0))
    c = GDN_CHUNK
    dk = x_ref.shape[-1]

    @pl.when(first)
    def _():
        s_ref[...] = s0_ref[0]

    for j in range(3 * GDN_HEADS):
        x = _conv_centred(x_ref[j], xprev_ref[j], xnext_ref[j], kpos, npos, cw_ref[j], cb_ref[j])
        x = jax.nn.silu(x)
        if j < 2 * GDN_HEADS:
            x = x * lax.rsqrt(jnp.sum(x * x, axis=-1, keepdims=True) + EPS)
        if j < GDN_HEADS:
            x = x * (dk ** -0.5)
        qkv_ref[j] = x
    ab = ab_ref[...]
    g_all = -jnp.exp(alog_ref[...]) * _softplus(ab + dtb_ref[...])
    lane_id = lax.broadcasted_iota(jnp.int32, ab.shape, 1)
    gate_ref[...] = jnp.where(lane_id < 2 * GDN_HEADS, g_all, jax.nn.sigmoid(ab))

    ri = lax.broadcasted_iota(jnp.int32, (c, c), 0)
    ci = lax.broadcasted_iota(jnp.int32, (c, c), 1)
    incl = (ri <= ci) if reverse else (ri >= ci)
    strict = (ri < ci) if reverse else (ri > ci)
    tri = jnp.where(incl, 1.0, 0.0).astype(F32)
    eye = jnp.where(ri == ci, 1.0, 0.0).astype(F32)
    blocks = []
    shift = 3
    while (1 << shift) < c:
        blocks.append((ri >> shift) == (ci >> shift))
        shift += 1
    end_row = 0 if reverse else c - 1
    seq_rows = lambda b: pl.ds(b, c, stride=SEQ_GROUP)
    keys = [(b, h) for b in range(SEQ_GROUP) for h in range(GDN_HEADS)]

    gcols, grams, decays, kbs, qes, betas = {}, {}, {}, {}, {}, {}
    for b in range(SEQ_GROUP):
        gates = gate_ref[seq_rows(b), :]
        gc = _dot_hi(tri, gates)
        gc_t = gc.T
        for h in range(GDN_HEADS):
            lane = direction * GDN_HEADS + h
            q = qkv_ref[h, seq_rows(b), :]
            k = qkv_ref[GDN_HEADS + h, seq_rows(b), :]
            beta = gates[:, 2 * GDN_HEADS + lane:2 * GDN_HEADS + lane + 1]
            gcol = gc[:, lane:lane + 1]
            grow = gc_t[lane:lane + 1, :]
            egc = jnp.exp(gcol)
            gcols[b, h] = gcol
            betas[b, h] = beta
            decays[b, h] = jnp.exp(jnp.where(incl, gcol - grow, -1e30))
            kb = k * beta
            kbs[b, h] = kb * egc
            qes[b, h] = q * egc
            grams[b, h] = _dot_nt(jnp.concatenate([kb, q], axis=0).astype(BF16), k.astype(BF16))
    lmats = [jnp.where(strict, grams[key][:c] * decays[key], 0.0) for key in keys]
    invs = _unit_triangular_inverses(lmats, eye, blocks)
    uws = {}
    for key, inv in zip(keys, invs):
        b, h = key
        vb = qkv_ref[2 * GDN_HEADS + h, seq_rows(b), :] * betas[key]
        rhs = jnp.concatenate([vb, kbs[key]], axis=1)
        uws[key] = _dot(inv.astype(BF16), rhs.astype(BF16))

    ss = {key: s_ref[key[0], key[1]] for key in keys}
    projs = {key: _dot(jnp.concatenate([uws[key][:, dk:], qes[key]], axis=0).astype(BF16),
                       ss[key].astype(BF16)) for key in keys}
    v_news = {key: uws[key][:, :dk] - projs[key][:c] for key in keys}
    for key in keys:
        b, h = key
        k = qkv_ref[GDN_HEADS + h, seq_rows(b), :]
        g_end = gcols[key][end_row:end_row + 1, :]
        kd = k * jnp.exp(g_end - gcols[key])
        s_ref[b, h] = ss[key] * jnp.exp(g_end) + _dot_tn(kd.astype(BF16),
                                                         v_news[key].astype(BF16))
    for key in keys:
        b, h = key
        qk = jnp.where(incl, grams[key][c:] * decays[key], 0.0)
        o_ref[h, seq_rows(b), :] = projs[key][c:] + _dot(qk.astype(BF16),
                                                         v_news[key].astype(BF16))

    @pl.when(last)
    def _():
        fin_ref[0] = s_ref[...]


def _gdn_scan(lay, qkv, ab, conv_w, conv_b, alog, dtb, s0, reverse, direction):
    slabs, rows, dk = qkv.shape
    sched = functools.partial(_schedule, lay, GDN_TILE_ROWS, reverse)
    tile3 = lambda i: (0, sched(i)[0], 0)
    tile2 = lambda i: (sched(i)[0], 0)
    grp = lambda i: (sched(i)[1], 0, 0, 0, 0)
    const2 = lambda a: pl.BlockSpec(a.shape, lambda i: (0, 0))
    const3 = lambda a: pl.BlockSpec(a.shape, lambda i: (0, 0, 0))
    prev_spec, next_spec = _halo_specs(lay, GDN_CHUNK, dk, reverse, slabs=slabs)
    sblock = (1,) + s0.shape[1:]
    return pl.pallas_call(
        functools.partial(_gdn_kernel, lay=lay, reverse=reverse, direction=direction),
        name="gdn_scan",
        grid=(lay.tiles(GDN_TILE_ROWS),),
        in_specs=[pl.BlockSpec((slabs, GDN_TILE_ROWS, dk), tile3), prev_spec, next_spec,
                  pl.BlockSpec((GDN_TILE_ROWS, LANES), tile2),
                  const3(conv_w), const3(conv_b), const2(alog), const2(dtb),
                  pl.BlockSpec(sblock, grp)],
        out_specs=[pl.BlockSpec((GDN_HEADS, GDN_TILE_ROWS, dk), tile3), pl.BlockSpec(sblock, grp)],
        out_shape=[jax.ShapeDtypeStruct((GDN_HEADS, rows, dk), F32),
                   jax.ShapeDtypeStruct(s0.shape, F32)],
        scratch_shapes=[pltpu.VMEM(s0.shape[1:], F32),
                        pltpu.VMEM((slabs, GDN_TILE_ROWS, dk), F32),
                        pltpu.VMEM((GDN_TILE_ROWS, LANES), F32)],
        compiler_params=_params(),
    )(qkv, qkv, qkv, ab, conv_w, conv_b, alog, dtb, s0)


def _to_time_major(x):
    b, t, c = x.shape
    return x.reshape(b // SEQ_GROUP, SEQ_GROUP, t, c).transpose(0, 2, 1, 3).reshape(b * t, c)


def _from_time_major(x, b, t):
    c = x.shape[-1]
    return x.reshape(b // SEQ_GROUP, t, SEQ_GROUP, c).transpose(0, 2, 1, 3).reshape(b, t, c)


def _group_states(lat_state, n_ctx_groups, flatten=True):
    db = lat_state.shape[0]
    tail = (-1,) if flatten else lat_state.shape[1:]
    grouped = lat_state.reshape((db // SEQ_GROUP, SEQ_GROUP) + tail).astype(F32)
    zeros = jnp.zeros((n_ctx_groups,) + grouped.shape[1:], F32)
    return jnp.concatenate([zeros, grouped], axis=0)


def kernel(x_prompt, x_sample, state_s5_re, state_s5_im, state_delta, state_lru, c, c_ctx, w_ada, b_ada, norm_mix_pre, norm_mix_post, norm_mlp_pre, norm_mlp_post, w_mlp_in, w_mlp_out, w_in_even, w_out_even, s5_lam_re, s5_lam_im, s5_log_dt, s5_b_re, s5_b_im, s5_c_re, s5_c_im, s5_d, gdn_conv_w, gdn_conv_b, gdn_a_log, gdn_dt_bias, gdn_o_norm, w_in_odd, w_out_odd, lru_conv_w, lru_conv_b, lru_w_r, lru_b_r, lru_w_i, lru_b_i, lru_lam):
    bp, tp, d = x_prompt.shape
    bl, tl, _ = x_sample.shape
    depth = w_ada.shape[0]
    n_dir = 2
    da = s5_d.shape[1]
    db = gdn_conv_w.shape[2] // 3
    heads = GDN_HEADS
    ngrp = s5_lam_re.shape[2]

    lay = Layout(bp // SEQ_GROUP, tp * SEQ_GROUP, bl // SEQ_GROUP, tl * SEQ_GROUP)

    cond = jnp.concatenate([c_ctx[None].astype(F32), c.astype(F32)], axis=0)
    pad = (-cond.shape[0]) % SUBLANES
    cond = jnp.pad(cond, ((0, pad), (0, 0)))
    ada = _ada_vectors(cond, w_ada, b_ada)
    mod_ctx = jnp.broadcast_to(ada[:, 0:1], (depth, SEQ_GROUP, ada.shape[-1]))[:, None]
    mod_lat = ada[:, 1:1 + bl].reshape(depth, bl // SEQ_GROUP, SEQ_GROUP, -1)
    mod = jnp.concatenate([mod_ctx, mod_lat], axis=1)

    x_lat = _add_positions(x_sample, _grid_sincos_table(tl, d))
    x = jnp.concatenate([_to_time_major(x_prompt), _to_time_major(x_lat)], axis=0)

    a_re, a_im, bb_re, bb_im = _s5_discretise(s5_lam_re, s5_lam_im, s5_log_dt, s5_b_re, s5_b_im)
    c_re_t = s5_c_re
    c_im_t = s5_c_im

    row = lambda v: v.reshape(1, -1).astype(F32)
    new_re, new_im, new_delta, new_lru = [], [], [], []
    for l in range(depth):
        mod_l = mod[l]
        if l % 2 == 0:
            e = l // 2
            n_in = w_in_even.shape[2]
            n_pad = (-n_in) % LANES
            w_in = jnp.pad(w_in_even[e], ((0, 0), (0, n_pad))).astype(BF16)
            splits = ((0, da, False), (da, 2 * da, False), (2 * da, 2 * da + 3 * db, True),
                      (2 * da + 3 * db, 2 * da + 4 * db, False),
                      (2 * da + 4 * db, n_in + n_pad, False))
            u, za, qkv, zb, ab = _pre_mixer(lay, x, mod_l, row(norm_mix_pre[l]), w_in, splits)

            cw = jnp.stack([_block_diag_out(c_re_t[e]), _block_diag_out(c_im_t[e])]).astype(BF16)
            ys, fr, fi = [], [], []
            for dd in range(n_dir):
                a8 = jnp.stack([a_re[e, dd].reshape(-1), a_im[e, dd].reshape(-1)])
                a8 = jnp.broadcast_to(a8[:, None, :], (2, SEQ_GROUP, a8.shape[-1]))
                bw = jnp.concatenate([_block_diag_in(bb_re[e, dd]), _block_diag_in(bb_im[e, dd])],
                                     axis=-1).astype(BF16)
                y, f_re, f_im = _s5_scan(lay, u, _group_states(state_s5_re[:, e, dd], lay.n_ctx),
                                         _group_states(state_s5_im[:, e, dd], lay.n_ctx),
                                         a8, bw, cw, reverse=(dd == 1))
                ys.append(y)
                fr.append(f_re[:lay.n_ctx].reshape(bp, ngrp, -1))
                fi.append(f_im[:lay.n_ctx].reshape(bp, ngrp, -1))
            new_re.append(jnp.stack(fr, axis=1))
            new_im.append(jnp.stack(fi, axis=1))

            pad8 = lambda v: jnp.pad(v.reshape(1, -1).astype(F32), ((0, 0), (0, LANES - v.size)))
            n_slabs = 3 * db // LANES
            conv_w = gdn_conv_w[e].astype(F32).reshape(CONV_K, n_slabs, LANES).transpose(1, 0, 2)
            conv_b = gdn_conv_b[e].astype(F32).reshape(n_slabs, 1, LANES)
            os_, fd = [], []
            for dd in range(n_dir):
                s0 = _group_states(state_delta[:, e, dd], lay.n_ctx, flatten=False)
                o_d, s_fin = _gdn_scan(lay, qkv, ab, conv_w, conv_b, pad8(gdn_a_log[e]),
                                       pad8(gdn_dt_bias[e]), s0, reverse=(dd == 1), direction=dd)
                os_.append(o_d)
                fd.append(s_fin[:lay.n_ctx].reshape((bp,) + s_fin.shape[2:]))
            new_delta.append(jnp.stack(fd, axis=1))

            x = _post_even(lay, u, za, ys[0], ys[1], os_[0], os_[1], zb, x, mod_l,
                           row(s5_d[e]), row(gdn_o_norm[e]), row(norm_mix_post[l]),
                           w_out_even[e].astype(BF16))
        else:
            o = l // 2
            dr = lru_lam.shape[2]
            w_in = w_in_odd[o].astype(BF16)
            xb, yg = _pre_mixer(lay, x, mod_l, row(norm_mix_pre[l]), w_in,
                                ((0, dr, False), (dr, 2 * dr, False)))
            hs, fl = [], []
            for dd in range(n_dir):
                h, h_fin = _lru_scan(lay, xb, _group_states(state_lru[:, o, dd], lay.n_ctx),
                                     lru_conv_w[o].astype(F32), row(lru_conv_b[o]),
                                     lru_w_r[o, dd].astype(BF16), lru_w_i[o, dd].astype(BF16),
                                     row(lru_b_r[o, dd]), row(lru_b_i[o, dd]), row(lru_lam[o, dd]),
                                     reverse=(dd == 1))
                hs.append(h)
                fl.append(h_fin[:lay.n_ctx].reshape(bp, dr))
            new_lru.append(jnp.stack(fl, axis=1))
            x = _post_odd(lay, hs[0], hs[1], yg, x, mod_l, row(norm_mix_post[l]),
                          w_out_odd[o].astype(BF16))

        x = _mlp(lay, x, mod_l, row(norm_mlp_pre[l]), row(norm_mlp_post[l]),
                 w_mlp_in[l].astype(BF16), w_mlp_out[l].astype(BF16))

    y_prompt = _from_time_major(x[:lay.ctx_rows], bp, tp).astype(x_prompt.dtype)
    y_sample = _from_time_major(x[lay.ctx_rows:], bl, tl).astype(x_sample.dtype)
    p_a = s5_lam_re.shape[3]
    new_s5_re = jnp.stack(new_re, axis=1).reshape(bp, -1, n_dir, ngrp, p_a)
    new_s5_im = jnp.stack(new_im, axis=1).reshape(bp, -1, n_dir, ngrp, p_a)
    return (y_prompt, y_sample, new_s5_re, new_s5_im,
            jnp.stack(new_delta, axis=1), jnp.stack(new_lru, axis=1))
```

```python
import functools
from typing import NamedTuple

import jax
import jax.numpy as jnp
from jax import lax
from jax.experimental import pallas as pl
from jax.experimental.pallas import tpu as pltpu

F32 = jnp.float32
BF16 = jnp.bfloat16
HIGHEST = lax.Precision.HIGHEST

EPS = 1e-6
POS_BASE = 10000.0
GRID_W = 64
N_MOD = 6
S5_GROUP = 16
S5_STATES = 64
GDN_HEADS = 4
GDN_CHUNK = 64
CONV_K = 4
CONV_LEFT = (CONV_K - 1) // 2
LRU_BLOCKS = 4
LRU_C = 8.0

SUBLANES = 8
LANES = 128
SEQ_GROUP = SUBLANES
SCAN_STEPS = 32
SCAN_ROWS = SCAN_STEPS * SEQ_GROUP
GDN_TILE_ROWS = GDN_CHUNK * SEQ_GROUP
TOKEN_ROWS = 512
MLP_HIDDEN_BLOCK = 2048
S5_SLICE = 128
VMEM_LIMIT_BYTES = 56 * 1024 * 1024


class Layout(NamedTuple):
    n_ctx: int
    len_ctx: int
    n_lat: int
    len_lat: int

    @property
    def ctx_rows(self):
        return self.n_ctx * self.len_ctx

    @property
    def rows(self):
        return self.ctx_rows + self.n_lat * self.len_lat

    @property
    def groups(self):
        return self.n_ctx + self.n_lat

    def tiles(self, tile_rows):
        return self.ctx_rows // tile_rows + self.n_lat * (self.len_lat // tile_rows)


def _schedule(lay, tile_rows, reverse, i):
    nc = lay.len_ctx // tile_rows
    nl = lay.len_lat // tile_rows
    n_ctx_tiles = lay.n_ctx * nc
    is_ctx = i < n_ctx_tiles
    j = jnp.maximum(i - n_ctx_tiles, 0)
    ic = jnp.minimum(i, n_ctx_tiles - 1)
    g = jnp.where(is_ctx, ic // nc, lay.n_ctx + j // nl)
    step = jnp.where(is_ctx, ic % nc, j % nl)
    n = jnp.where(is_ctx, nc, nl)
    k = (n - 1 - step) if reverse else step
    base = jnp.where(is_ctx, (ic // nc) * nc, n_ctx_tiles + (j // nl) * nl)
    return base + k, g, step == 0, step == n - 1, k, n


def _params(n_axes=1):
    return pltpu.CompilerParams(dimension_semantics=("arbitrary",) * n_axes,
                                vmem_limit_bytes=VMEM_LIMIT_BYTES)


def _dot(a, b):
    return jnp.dot(a, b, preferred_element_type=F32)


def _dot_hi(a, b):
    return jnp.dot(a, b, precision=HIGHEST, preferred_element_type=F32)


def _dot_nt(a, b):
    return lax.dot_general(a, b, (((1,), (1,)), ((), ())), preferred_element_type=F32)


def _dot_tn(a, b):
    return lax.dot_general(a, b, (((0,), (0,)), ((), ())), preferred_element_type=F32)


def _rms(x):
    return x * lax.rsqrt(jnp.mean(x * x, axis=-1, keepdims=True) + EPS)


def _per_group(rows_val, vec8):
    r, c = rows_val.shape
    return rows_val.reshape(r // SEQ_GROUP, SEQ_GROUP, c), vec8[None]


def _modulate(h, shift8, scale8):
    h3, sc = _per_group(h, scale8)
    out = h3 * (1.0 + sc) + shift8[None]
    return out.reshape(h.shape)


def _gated_residual(x, y, gate8):
    y3, g = _per_group(y, gate8)
    return x + (y3 * g).reshape(x.shape)


def _mod_chunk(mod_ref, idx, d):
    return mod_ref[0, :, idx * d:(idx + 1) * d]


def _softplus(x):
    return jnp.maximum(x, 0.0) + jnp.log1p(jnp.exp(-jnp.abs(x)))


def _sigmoid(x):
    return 0.5 * jnp.tanh(0.5 * x) + 0.5


def _ada_kernel(c_ref, w_ref, b_ref, o_ref):
    s = jax.nn.silu(c_ref[...]).astype(BF16)
    o_ref[0] = _dot(s, w_ref[0].astype(BF16)) + b_ref[0]


def _ada_vectors(cond, w_ada, b_ada):
    depth, d, n = w_ada.shape
    rows = cond.shape[0]
    tn = 512
    return pl.pallas_call(
        _ada_kernel, name="ada_vectors",
        grid=(depth, n // tn),
        in_specs=[pl.BlockSpec((rows, d), lambda l, j: (0, 0)),
                  pl.BlockSpec((1, d, tn), lambda l, j: (l, 0, j)),
                  pl.BlockSpec((1, 1, tn), lambda l, j: (l, 0, j))],
        out_specs=pl.BlockSpec((1, rows, tn), lambda l, j: (l, 0, j)),
        out_shape=jax.ShapeDtypeStruct((depth, rows, n), F32),
        compiler_params=_params(2),
    )(cond, w_ada, b_ada.reshape(depth, 1, n))


def _pos_kernel(x_ref, p_ref, o_ref):
    o_ref[0] = x_ref[0] + p_ref[...]


def _grid_sincos_table(n_tokens, d_model):
    rows = n_tokens // GRID_W
    row = jnp.repeat(jnp.arange(rows, dtype=F32), GRID_W)
    col = jnp.tile(jnp.arange(GRID_W, dtype=F32), rows)
    n_freq = d_model // 4
    omega = POS_BASE ** (-jnp.arange(n_freq, dtype=F32) / n_freq)
    ar = row[:, None] * omega
    ac = col[:, None] * omega
    return jnp.concatenate([jnp.sin(ar), jnp.cos(ar), jnp.sin(ac), jnp.cos(ac)], axis=-1)


def _add_positions(x, table):
    b, t, d = x.shape
    tt = min(t, 512)
    return pl.pallas_call(
        _pos_kernel, name="add_positions",
        grid=(b, t // tt),
        in_specs=[pl.BlockSpec((1, tt, d), lambda i, j: (i, j, 0)),
                  pl.BlockSpec((tt, d), lambda i, j: (j, 0))],
        out_specs=pl.BlockSpec((1, tt, d), lambda i, j: (i, j, 0)),
        out_shape=jax.ShapeDtypeStruct(x.shape, x.dtype),
        compiler_params=_params(2),
    )(x, table)


def _mod_index(lay, tile_rows):
    def index(i):
        row = i * tile_rows
        lat = jnp.maximum(row - lay.ctx_rows, 0) // lay.len_lat
        return jnp.where(row < lay.ctx_rows, 0, 1 + lat)
    return index


def _pre_kernel(x_ref, mod_ref, gain_ref, w_ref, *out_refs, splits):
    d = x_ref.shape[-1]
    h = _rms(x_ref[...]) * gain_ref[...]
    h = _modulate(h, _mod_chunk(mod_ref, 0, d), _mod_chunk(mod_ref, 1, d)).astype(BF16)
    for (a, b, slabs), o_ref in zip(splits, out_refs):
        if slabs:
            acc = _dot(h, w_ref[:, a:b])
            for j in range((b - a) // LANES):
                o_ref[j] = acc[:, j * LANES:(j + 1) * LANES]
        else:
            o_ref[...] = _dot(h, w_ref[:, a:b])


def _pre_mixer(lay, x, mod, gain, w, splits):
    rows, d = x.shape
    n = w.shape[1]
    midx = _mod_index(lay, TOKEN_ROWS)
    out_specs, out_shapes = [], []
    for a, b, slabs in splits:
        if slabs:
            ns = (b - a) // LANES
            out_specs.append(pl.BlockSpec((ns, TOKEN_ROWS, LANES), lambda i: (0, i, 0)))
            out_shapes.append(jax.ShapeDtypeStruct((ns, rows, LANES), F32))
        else:
            out_specs.append(pl.BlockSpec((TOKEN_ROWS, b - a), lambda i: (i, 0)))
            out_shapes.append(jax.ShapeDtypeStruct((rows, b - a), F32))
    return pl.pallas_call(
        functools.partial(_pre_kernel, splits=splits), name="pre_mixer",
        grid=(rows // TOKEN_ROWS,),
        in_specs=[pl.BlockSpec((TOKEN_ROWS, d), lambda i: (i, 0)),
                  pl.BlockSpec((1, SEQ_GROUP, N_MOD * d), lambda i: (midx(i), 0, 0)),
                  pl.BlockSpec((1, d), lambda i: (0, 0)),
                  pl.BlockSpec((d, n), lambda i: (0, 0))],
        out_specs=out_specs, out_shape=out_shapes,
        compiler_params=_params(),
    )(x, mod, gain, w)


def _finish_sublayer(x, out, gain, gate8):
    return _gated_residual(x, _rms(out) * gain, gate8)


def _mlp_sublayer(x, mod_ref, gpre_ref, gpost_ref, w1_ref, w2_ref):
    d = x.shape[-1]
    dff = w1_ref.shape[1]
    h = _rms(x) * gpre_ref[...]
    h = _modulate(h, _mod_chunk(mod_ref, 3, d), _mod_chunk(mod_ref, 4, d)).astype(BF16)
    out = None
    for a in range(0, dff, MLP_HIDDEN_BLOCK):
        f = jnp.square(jnp.maximum(_dot(h, w1_ref[:, a:a + MLP_HIDDEN_BLOCK]), 0.0))
        part = _dot(f.astype(BF16), w2_ref[a:a + MLP_HIDDEN_BLOCK, :])
        out = part if out is None else out + part
    return _finish_sublayer(x, out, gpost_ref[...], _mod_chunk(mod_ref, 5, d))


def _tail_odd_kernel(hf_ref, hb_ref, yg_ref, x_ref, mod_ref, gmix_ref, wout_ref,
                     gpre_ref, gpost_ref, w1_ref, w2_ref, o_ref):
    d = x_ref.shape[-1]
    y = (hf_ref[...] + hb_ref[...]) * jax.nn.gelu(yg_ref[...])
    out = _dot(y.astype(BF16), wout_ref[...])
    x = _finish_sublayer(x_ref[...], out, gmix_ref[...], _mod_chunk(mod_ref, 2, d))
    o_ref[...] = _mlp_sublayer(x, mod_ref, gpre_ref, gpost_ref, w1_ref, w2_ref)


def _tail_even_kernel(u_ref, za_ref, yf_ref, yb_ref, of_ref, ob_ref, zb_ref, x_ref, mod_ref,
                      dskip_ref, onorm_ref, gmix_ref, wout_ref,
                      gpre_ref, gpost_ref, w1_ref, w2_ref, o_ref):
    d = x_ref.shape[-1]
    da = u_ref.shape[-1]
    heads, _, dv = of_ref.shape
    ya = jax.nn.gelu(yf_ref[...] + yb_ref[...] + u_ref[...] * dskip_ref[...])
    ya = ya * jax.nn.sigmoid(za_ref[...])
    out = _dot(ya.astype(BF16), wout_ref[0:da, :])
    for h in range(heads):
        o = of_ref[h] + ob_ref[h]
        oh = _rms(o) * onorm_ref[...] * jax.nn.silu(zb_ref[:, h * dv:(h + 1) * dv])
        out = out + _dot(oh.astype(BF16), wout_ref[da + h * dv:da + (h + 1) * dv, :])
    x = _finish_sublayer(x_ref[...], out, gmix_ref[...], _mod_chunk(mod_ref, 2, d))
    o_ref[...] = _mlp_sublayer(x, mod_ref, gpre_ref, gpost_ref, w1_ref, w2_ref)


def _layer_tail(lay, body, name, acts, x, mod, small, wout, gpre, gpost, w1, w2):
    rows, d = x.shape
    midx = _mod_index(lay, TOKEN_ROWS)

    def act_spec(a):
        if a.ndim == 3:
            return pl.BlockSpec((a.shape[0], TOKEN_ROWS, a.shape[2]), lambda i: (0, i, 0))
        return pl.BlockSpec((TOKEN_ROWS, a.shape[1]), lambda i: (i, 0))

    const = lambda a: pl.BlockSpec(a.shape, lambda i: (0, 0))
    resident = lambda a: pl.BlockSpec(a.shape, lambda i: (0, 0), pipeline_mode=pl.Buffered(1))
    return pl.pallas_call(
        body, name=name,
        grid=(rows // TOKEN_ROWS,),
        in_specs=[act_spec(a) for a in acts] + [
            act_spec(x), pl.BlockSpec((1, SEQ_GROUP, N_MOD * d), lambda i: (midx(i), 0, 0))]
            + [const(s) for s in small] + [resident(wout), const(gpre), const(gpost),
                                           resident(w1), resident(w2)],
        out_specs=act_spec(x),
        out_shape=jax.ShapeDtypeStruct((rows, d), F32),
        compiler_params=_params(),
    )(*acts, x, mod, *small, wout, gpre, gpost, w1, w2)


def _s5_discretise_kernel(lr_ref, li_ref, ldt_ref, bre_ref, bim_ref,
                          ar_ref, ai_ref, bbre_ref, bbim_ref):
    lr, li = lr_ref[...], li_ref[...]
    dt = jnp.exp(ldt_ref[...])
    mag = jnp.exp(lr * dt)
    ar = mag * jnp.cos(li * dt)
    ai = mag * jnp.sin(li * dt)
    den = lr * lr + li * li
    fr = ((ar - 1.0) * lr + ai * li) / den
    fi = (ai * lr - (ar - 1.0) * li) / den
    ar_ref[...] = ar
    ai_ref[...] = ai
    bbre_ref[...] = fr * bre_ref[...] - fi * bim_ref[...]
    bbim_ref[...] = fr * bim_ref[...] + fi * bre_ref[...]


def _s5_discretise(lam_re, lam_im, log_dt, b_re, b_im):
    e, nd, g, p = lam_re.shape
    c = b_re.shape[-1]
    full = (e, nd, g, c, p)
    flat = lambda a: jnp.broadcast_to(a, full).reshape(-1, p)
    lr = flat(lam_re[:, :, :, None, :])
    li = flat(lam_im[:, :, :, None, :])
    ldt = flat(log_dt[:, :, :, None, None])
    bre = flat(b_re.transpose(0, 1, 3, 2)[:, None])
    bim = flat(b_im.transpose(0, 1, 3, 2)[:, None])
    shape = jax.ShapeDtypeStruct(lr.shape, F32)
    ar, ai, bbre, bbim = pl.pallas_call(
        _s5_discretise_kernel, name="s5_discretise", out_shape=[shape] * 4,
        compiler_params=pltpu.CompilerParams(vmem_limit_bytes=VMEM_LIMIT_BYTES),
    )(lr, li, ldt, bre, bim)
    ar = ar.reshape(full)[:, :, :, 0, :]
    ai = ai.reshape(full)[:, :, :, 0, :]
    return ar, ai, bbre.reshape(full), bbim.reshape(full)


def _block_diag_in(bb):
    g, c, p = bb.shape
    per = S5_SLICE // c
    n = g // per
    blocks = bb.reshape(n, per, c, p)
    eye = jnp.eye(per, dtype=bb.dtype)
    return jnp.einsum('nicp,ij->nicjp', blocks, eye).reshape(n, per * c, per * p)


def _block_diag_out(cc):
    g, c, p = cc.shape
    per = S5_SLICE // c
    n = g // per
    blocks = cc.reshape(n, per, c, p)
    eye = jnp.eye(per, dtype=cc.dtype)
    return jnp.einsum('nicp,ij->nipjc', blocks, eye).reshape(n, per * p, per * c)


def _s5_scan_kernel(u_ref, h0re_ref, h0im_ref, a_ref, bw_ref, cw_ref,
                    y_ref, fre_ref, fim_ref, hre_ref, him_ref, bre_ref, bim_ref,
                    *, lay, reverse):
    _, grp, first, last, _, _ = _schedule(lay, SCAN_ROWS, reverse, pl.program_id(0))
    is_ctx = grp < lay.n_ctx
    n_slices = bw_ref.shape[0]
    wide = bw_ref.shape[2] // 2

    @pl.when(first)
    def _():
        hre_ref[...] = jnp.where(is_ctx, 0.0, h0re_ref[0])
        him_ref[...] = jnp.where(is_ctx, 0.0, h0im_ref[0])

    for j in range(n_slices):
        uj = u_ref[:, j * S5_SLICE:(j + 1) * S5_SLICE].astype(BF16)
        bb = _dot(uj, bw_ref[j])
        bre_ref[:, j * wide:(j + 1) * wide] = bb[:, :wide]
        bim_ref[:, j * wide:(j + 1) * wide] = bb[:, wide:]

    steps = range(SCAN_STEPS - 1, -1, -1) if reverse else range(SCAN_STEPS)
    for j in range(n_slices):
        cols = slice(j * wide, (j + 1) * wide)
        ar = a_ref[0, :, cols]
        ai = a_ref[1, :, cols]
        hr = hre_ref[:, cols]
        hi = him_ref[:, cols]
        for t in steps:
            rows = slice(t * SEQ_GROUP, (t + 1) * SEQ_GROUP)
            nr = ar * hr - ai * hi + bre_ref[rows, cols]
            ni = ar * hi + ai * hr + bim_ref[rows, cols]
            bre_ref[rows, cols] = nr
            bim_ref[rows, cols] = ni
            hr, hi = nr, ni
        hre_ref[:, cols] = hr
        him_ref[:, cols] = hi

    for j in range(n_slices):
        cols = slice(j * wide, (j + 1) * wide)
        yj = _dot(bre_ref[:, cols].astype(BF16), cw_ref[0, j])
        yj = yj - _dot(bim_ref[:, cols].astype(BF16), cw_ref[1, j])
        y_ref[:, j * S5_SLICE:(j + 1) * S5_SLICE] = yj

    @pl.when(last & is_ctx)
    def _():
        fre_ref[0] = hre_ref[...]
        fim_ref[0] = him_ref[...]


def _state_maps(lay, sched, ndim):
    pad = (0,) * (ndim - 1)
    start = lambda i: (jnp.maximum(sched(i)[1] - lay.n_ctx, 0),) + pad
    final = lambda i: (jnp.minimum(sched(i)[1], lay.n_ctx - 1),) + pad
    return start, final


def _s5_scan(lay, u, h0re, h0im, a8, bw, cw, reverse):
    rows, da = u.shape
    ns = h0re.shape[-1]
    sched = functools.partial(_schedule, lay, SCAN_ROWS, reverse)
    tile = lambda i: (sched(i)[0], 0)
    start, final = _state_maps(lay, sched, 3)
    state_shape = jax.ShapeDtypeStruct((lay.n_ctx, SEQ_GROUP, ns), F32)
    return pl.pallas_call(
        functools.partial(_s5_scan_kernel, lay=lay, reverse=reverse), name="s5_scan",
        grid=(lay.tiles(SCAN_ROWS),),
        in_specs=[pl.BlockSpec((SCAN_ROWS, da), tile),
                  pl.BlockSpec((1, SEQ_GROUP, ns), start),
                  pl.BlockSpec((1, SEQ_GROUP, ns), start),
                  pl.BlockSpec(a8.shape, lambda i: (0, 0, 0)),
                  pl.BlockSpec(bw.shape, lambda i: (0, 0, 0)),
                  pl.BlockSpec(cw.shape, lambda i: (0, 0, 0, 0))],
        out_specs=[pl.BlockSpec((SCAN_ROWS, da), tile),
                   pl.BlockSpec((1, SEQ_GROUP, ns), final),
                   pl.BlockSpec((1, SEQ_GROUP, ns), final)],
        out_shape=[jax.ShapeDtypeStruct((rows, da), F32), state_shape, state_shape],
        scratch_shapes=[pltpu.VMEM((SEQ_GROUP, ns), F32), pltpu.VMEM((SEQ_GROUP, ns), F32),
                        pltpu.VMEM((SCAN_ROWS, ns), F32), pltpu.VMEM((SCAN_ROWS, ns), F32)],
        compiler_params=_params(),
    )(u, h0re, h0im, a8, bw, cw)


def _halo_specs(lay, steps, channels, reverse=False, slabs=None):
    sched = functools.partial(_schedule, lay, steps * SEQ_GROUP, reverse)
    last_pair = lay.rows // (2 * SEQ_GROUP) - 1
    lead = () if slabs is None else (0,)
    shape = () if slabs is None else (slabs,)

    def prev(i):
        t = sched(i)[0]
        return lead + (jnp.maximum(t * steps - 1, 0), 0)

    def nxt(i):
        t = sched(i)[0]
        return lead + (jnp.minimum((t + 1) * (steps // 2), last_pair), 0)

    return (pl.BlockSpec(shape + (SEQ_GROUP, channels), prev),
            pl.BlockSpec(shape + (2 * SEQ_GROUP, channels), nxt))


def _conv_centred(x, prev, nxt, k, n, w, b):
    prev = jnp.where(k > 0, prev, 0.0)
    nxt = jnp.where(k < n - 1, nxt, 0.0)
    xp = jnp.concatenate([prev, x, nxt], axis=0)
    r = x.shape[0]
    out = b + xp[0:r] * w[0:1, :]
    for j in range(1, CONV_K):
        out = out + xp[j * SEQ_GROUP:j * SEQ_GROUP + r] * w[j:j + 1, :]
    return out


def _lru_scan_kernel(x_ref, xprev_ref, xnext_ref, h0_ref, cw_ref, cb_ref, wr_ref, wi_ref,
                     br_ref, bi_ref, lam_ref, h_ref, fin_ref, st_ref, a_ref, b_ref,
                     *, lay, reverse):
    _, grp, first, last, k, n = _schedule(lay, SCAN_ROWS, reverse, pl.program_id(0))
    is_ctx = grp < lay.n_ctx

    @pl.when(first)
    def _():
        st_ref[...] = jnp.where(is_ctx, 0.0, h0_ref[0])

    x = _conv_centred(x_ref[...], xprev_ref[...], xnext_ref[...], k, n, cw_ref[...], cb_ref[...])
    xb = x.astype(BF16)
    bs = wr_ref.shape[-1]
    neg_sp = -LRU_C * _softplus(-lam_ref[...])
    for blk in range(LRU_BLOCKS):
        cols = slice(blk * bs, (blk + 1) * bs)
        r = _sigmoid(_dot(xb[:, cols], wr_ref[blk]) + br_ref[:, cols])
        gi = _sigmoid(_dot(xb[:, cols], wi_ref[blk]) + bi_ref[:, cols])
        log_a = neg_sp[:, cols] * r
        th = jnp.tanh(log_a)
        a_ref[:, cols] = jnp.exp(log_a)
        b_ref[:, cols] = jnp.sqrt(-2.0 * th / (1.0 - th)) * (gi * x[:, cols])

    steps = range(SCAN_STEPS - 1, -1, -1) if reverse else range(SCAN_STEPS)
    h = st_ref[...]
    for t in steps:
        rows = slice(t * SEQ_GROUP, (t + 1) * SEQ_GROUP)
        h = a_ref[rows, :] * h + b_ref[rows, :]
        h_ref[rows, :] = h
    st_ref[...] = h

    @pl.when(last & is_ctx)
    def _():
        fin_ref[0] = st_ref[...]


def _lru_scan(lay, xb, h0, conv_w, conv_b, wr, wi, br, bi, lam, reverse):
    rows, d = xb.shape
    sched = functools.partial(_schedule, lay, SCAN_ROWS, reverse)
    tile = lambda i: (sched(i)[0], 0)
    start, final = _state_maps(lay, sched, 3)
    const2 = lambda a: pl.BlockSpec(a.shape, lambda i: (0, 0))
    const3 = lambda a: pl.BlockSpec(a.shape, lambda i: (0, 0, 0))
    prev_spec, next_spec = _halo_specs(lay, SCAN_STEPS, d, reverse)
    return pl.pallas_call(
        functools.partial(_lru_scan_kernel, lay=lay, reverse=reverse), name="lru_scan",
        grid=(lay.tiles(SCAN_ROWS),),
        in_specs=[pl.BlockSpec((SCAN_ROWS, d), tile), prev_spec, next_spec,
                  pl.BlockSpec((1, SEQ_GROUP, d), start),
                  const2(conv_w), const2(conv_b), const3(wr), const3(wi),
                  const2(br), const2(bi), const2(lam)],
        out_specs=[pl.BlockSpec((SCAN_ROWS, d), tile), pl.BlockSpec((1, SEQ_GROUP, d), final)],
        out_shape=[jax.ShapeDtypeStruct((rows, d), F32),
                   jax.ShapeDtypeStruct((lay.n_ctx, SEQ_GROUP, d), F32)],
        scratch_shapes=[pltpu.VMEM((SEQ_GROUP, d), F32),
                        pltpu.VMEM((SCAN_ROWS, d), F32), pltpu.VMEM((SCAN_ROWS, d), F32)],
        compiler_params=_params(),
    )(xb, xb, xb, h0, conv_w, conv_b, wr, wi, br, bi, lam)


def _unit_triangular_inverses(lmats, eye, blocks):
    c = lmats[0].shape[0]
    ts = [jnp.where(blocks[0], -l, 0.0) for l in lmats]
    tbs = [t.astype(BF16) for t in ts]
    invs = [eye + t for t in ts]
    ps = [_dot(tb, tb) for tb in tbs]
    ys = [_dot(jnp.concatenate([x, p], axis=0).astype(BF16), p.astype(BF16))
          for x, p in zip(invs, ps)]
    invs = [x + y[:c] for x, y in zip(invs, ys)]
    zs = [_dot(x.astype(BF16), y[c:].astype(BF16)) for x, y in zip(invs, ys)]
    invs = [x + z for x, z in zip(invs, zs)]
    inside = blocks[0]
    for outer in blocks[1:] + [None]:
        off = ~inside if outer is None else (outer & ~inside)
        es = [jnp.where(off, l, 0.0).astype(BF16) for l in lmats]
        xbs = [x.astype(BF16) for x in invs]
        xes = [_dot(xb, e).astype(BF16) for xb, e in zip(xbs, es)]
        xexs = [_dot(xe, xb) for xe, xb in zip(xes, xbs)]
        invs = [x - xex for x, xex in zip(invs, xexs)]
        inside = outer
    return invs


def _gdn_kernel(x_ref, xprev_ref, xnext_ref, ab_ref, cw_ref, cb_ref, alog_ref, dtb_ref, s0_ref,
                o_ref, fin_ref, s_ref, qkv_ref, gate_ref, *, lay, reverse, direction):
    _, grp, first, last, kpos, npos = _schedule(lay, GDN_TILE_ROWS, reverse, pl.program_id(0))
    is_ctx = grp < lay.n_ctx
    c = GDN_CHUNK
    dk = x_ref.shape[-1]

    @pl.when(first)
    def _():
        s_ref[...] = jnp.where(is_ctx, 0.0, s0_ref[0])

    for j in range(3 * GDN_HEADS):
        x = _conv_centred(x_ref[j], xprev_ref[j], xnext_ref[j], kpos, npos, cw_ref[j], cb_ref[j])
        x = jax.nn.silu(x)
        if j < 2 * GDN_HEADS:
            x = x * lax.rsqrt(jnp.sum(x * x, axis=-1, keepdims=True) + EPS)
        if j < GDN_HEADS:
            x = x * (dk ** -0.5)
        qkv_ref[j] = x
    ab = ab_ref[...]
    g_all = -jnp.exp(alog_ref[...]) * _softplus(ab + dtb_ref[...])
    lane_id = lax.broadcasted_iota(jnp.int32, ab.shape, 1)
    gate_ref[...] = jnp.where(lane_id < 2 * GDN_HEADS, g_all, jax.nn.sigmoid(ab))

    ri = lax.broadcasted_iota(jnp.int32, (c, c), 0)
    ci = lax.broadcasted_iota(jnp.int32, (c, c), 1)
    incl = (ri <= ci) if reverse else (ri >= ci)
    strict = (ri < ci) if reverse else (ri > ci)
    tri = jnp.where(incl, 1.0, 0.0).astype(F32)
    eye = jnp.where(ri == ci, 1.0, 0.0).astype(F32)
    blocks = []
    shift = 3
    while (1 << shift) < c:
        blocks.append((ri >> shift) == (ci >> shift))
        shift += 1
    end_row = 0 if reverse else c - 1
    seq_rows = lambda b: pl.ds(b, c, stride=SEQ_GROUP)
    keys = [(b, h) for b in range(SEQ_GROUP) for h in range(GDN_HEADS)]

    gcols, grams, decays, kbs, qes, betas = {}, {}, {}, {}, {}, {}
    for b in range(SEQ_GROUP):
        gates = gate_ref[seq_rows(b), :]
        gc = _dot_hi(tri, gates)
        gc_t = gc.T
        for h in range(GDN_HEADS):
            lane = direction * GDN_HEADS + h
            q = qkv_ref[h, seq_rows(b), :]
            k = qkv_ref[GDN_HEADS + h, seq_rows(b), :]
            beta = gates[:, 2 * GDN_HEADS + lane:2 * GDN_HEADS + lane + 1]
            gcol = gc[:, lane:lane + 1]
            grow = gc_t[lane:lane + 1, :]
            egc = jnp.exp(gcol)
            gcols[b, h] = gcol
            betas[b, h] = beta
            decays[b, h] = jnp.exp(jnp.where(incl, gcol - grow, -1e30))
            kb = k * beta
            kbs[b, h] = kb * egc
            qes[b, h] = q * egc
            grams[b, h] = _dot_nt(jnp.concatenate([kb, q], axis=0).astype(BF16), k.astype(BF16))
    lmats = [jnp.where(strict, grams[key][:c] * decays[key], 0.0) for key in keys]
    invs = _unit_triangular_inverses(lmats, eye, blocks)
    uws = {}
    for key, inv in zip(keys, invs):
        b, h = key
        vb = qkv_ref[2 * GDN_HEADS + h, seq_rows(b), :] * betas[key]
        rhs = jnp.concatenate([vb, kbs[key]], axis=1)
        uws[key] = _dot(inv.astype(BF16), rhs.astype(BF16))

    ss = {key: s_ref[key[0], key[1]] for key in keys}
    projs = {key: _dot(jnp.concatenate([uws[key][:, dk:], qes[key]], axis=0).astype(BF16),
                       ss[key].astype(BF16)) for key in keys}
    v_news = {key: uws[key][:, :dk] - projs[key][:c] for key in keys}
    for key in keys:
        b, h = key
        k = qkv_ref[GDN_HEADS + h, seq_rows(b), :]
        g_end = gcols[key][end_row:end_row + 1, :]
        kd = k * jnp.exp(g_end - gcols[key])
        s_ref[b, h] = ss[key] * jnp.exp(g_end) + _dot_tn(kd.astype(BF16),
                                                         v_news[key].astype(BF16))
    for key in keys:
        b, h = key
        qk = jnp.where(incl, grams[key][c:] * decays[key], 0.0)
        o_ref[h, seq_rows(b), :] = projs[key][c:] + _dot(qk.astype(BF16),
                                                         v_news[key].astype(BF16))

    @pl.when(last & is_ctx)
    def _():
        fin_ref[0] = s_ref[...]


def _gdn_scan(lay, qkv, ab, conv_w, conv_b, alog, dtb, s0, reverse, direction):
    slabs, rows, dk = qkv.shape
    sched = functools.partial(_schedule, lay, GDN_TILE_ROWS, reverse)
    tile3 = lambda i: (0, sched(i)[0], 0)
    tile2 = lambda i: (sched(i)[0], 0)
    start, final = _state_maps(lay, sched, 5)
    const2 = lambda a: pl.BlockSpec(a.shape, lambda i: (0, 0))
    const3 = lambda a: pl.BlockSpec(a.shape, lambda i: (0, 0, 0))
    prev_spec, next_spec = _halo_specs(lay, GDN_CHUNK, dk, reverse, slabs=slabs)
    sblock = (1,) + s0.shape[1:]
    return pl.pallas_call(
        functools.partial(_gdn_kernel, lay=lay, reverse=reverse, direction=direction),
        name="gdn_scan",
        grid=(lay.tiles(GDN_TILE_ROWS),),
        in_specs=[pl.BlockSpec((slabs, GDN_TILE_ROWS, dk), tile3), prev_spec, next_spec,
                  pl.BlockSpec((GDN_TILE_ROWS, LANES), tile2),
                  const3(conv_w), const3(conv_b), const2(alog), const2(dtb),
                  pl.BlockSpec(sblock, start)],
        out_specs=[pl.BlockSpec((GDN_HEADS, GDN_TILE_ROWS, dk), tile3),
                   pl.BlockSpec(sblock, final)],
        out_shape=[jax.ShapeDtypeStruct((GDN_HEADS, rows, dk), F32),
                   jax.ShapeDtypeStruct((lay.n_ctx,) + s0.shape[1:], F32)],
        scratch_shapes=[pltpu.VMEM(s0.shape[1:], F32),
                        pltpu.VMEM((slabs, GDN_TILE_ROWS, dk), F32),
                        pltpu.VMEM((GDN_TILE_ROWS, LANES), F32)],
        compiler_params=_params(),
    )(qkv, qkv, qkv, ab, conv_w, conv_b, alog, dtb, s0)


def _to_time_major(x):
    b, t, c = x.shape
    return x.reshape(b // SEQ_GROUP, SEQ_GROUP, t, c).transpose(0, 2, 1, 3).reshape(b * t, c)


def _from_time_major(x, b, t):
    c = x.shape[-1]
    return x.reshape(b // SEQ_GROUP, t, SEQ_GROUP, c).transpose(0, 2, 1, 3).reshape(b, t, c)


def _group_states(lat_state, flatten=True):
    db = lat_state.shape[0]
    tail = (-1,) if flatten else lat_state.shape[1:]
    return lat_state.reshape((db // SEQ_GROUP, SEQ_GROUP) + tail).astype(F32)


def kernel(x_prompt, x_sample, state_s5_re, state_s5_im, state_delta, state_lru, c, c_ctx, w_ada, b_ada, norm_mix_pre, norm_mix_post, norm_mlp_pre, norm_mlp_post, w_mlp_in, w_mlp_out, w_in_even, w_out_even, s5_lam_re, s5_lam_im, s5_log_dt, s5_b_re, s5_b_im, s5_c_re, s5_c_im, s5_d, gdn_conv_w, gdn_conv_b, gdn_a_log, gdn_dt_bias, gdn_o_norm, w_in_odd, w_out_odd, lru_conv_w, lru_conv_b, lru_w_r, lru_b_r, lru_w_i, lru_b_i, lru_lam):
    bp, tp, d = x_prompt.shape
    bl, tl, _ = x_sample.shape
    depth = w_ada.shape[0]
    n_dir = 2
    da = s5_d.shape[1]
    db = gdn_conv_w.shape[2] // 3
    heads = GDN_HEADS
    ngrp = s5_lam_re.shape[2]

    lay = Layout(bp // SEQ_GROUP, tp * SEQ_GROUP, bl // SEQ_GROUP, tl * SEQ_GROUP)

    cond = jnp.concatenate([c_ctx[None].astype(F32), c.astype(F32)], axis=0)
    pad = (-cond.shape[0]) % SUBLANES
    cond = jnp.pad(cond, ((0, pad), (0, 0)))
    ada = _ada_vectors(cond, w_ada, b_ada)
    mod_ctx = jnp.broadcast_to(ada[:, 0:1], (depth, SEQ_GROUP, ada.shape[-1]))[:, None]
    mod_lat = ada[:, 1:1 + bl].reshape(depth, bl // SEQ_GROUP, SEQ_GROUP, -1)
    mod = jnp.concatenate([mod_ctx, mod_lat], axis=1)

    x_lat = _add_positions(x_sample, _grid_sincos_table(tl, d))
    x = jnp.concatenate([_to_time_major(x_prompt), _to_time_major(x_lat)], axis=0)

    a_re, a_im, bb_re, bb_im = _s5_discretise(s5_lam_re, s5_lam_im, s5_log_dt, s5_b_re, s5_b_im)
    c_re_t = s5_c_re
    c_im_t = s5_c_im

    row = lambda v: v.reshape(1, -1).astype(F32)
    new_re, new_im, new_delta, new_lru = [], [], [], []
    for l in range(depth):
        mod_l = mod[l]
        if l % 2 == 0:
            e = l // 2
            n_in = w_in_even.shape[2]
            n_pad = (-n_in) % LANES
            w_in = jnp.pad(w_in_even[e], ((0, 0), (0, n_pad))).astype(BF16)
            splits = ((0, da, False), (da, 2 * da, False), (2 * da, 2 * da + 3 * db, True),
                      (2 * da + 3 * db, 2 * da + 4 * db, False),
                      (2 * da + 4 * db, n_in + n_pad, False))
            u, za, qkv, zb, ab = _pre_mixer(lay, x, mod_l, row(norm_mix_pre[l]), w_in, splits)

            cw = jnp.stack([_block_diag_out(c_re_t[e]), _block_diag_out(c_im_t[e])]).astype(BF16)
            ys, fr, fi = [], [], []
            for dd in range(n_dir):
                a8 = jnp.stack([a_re[e, dd].reshape(-1), a_im[e, dd].reshape(-1)])
                a8 = jnp.broadcast_to(a8[:, None, :], (2, SEQ_GROUP, a8.shape[-1]))
                bw = jnp.concatenate([_block_diag_in(bb_re[e, dd]), _block_diag_in(bb_im[e, dd])],
                                     axis=-1).astype(BF16)
                y, f_re, f_im = _s5_scan(lay, u, _group_states(state_s5_re[:, e, dd]),
                                         _group_states(state_s5_im[:, e, dd]),
                                         a8, bw, cw, reverse=(dd == 1))
                ys.append(y)
                fr.append(f_re.reshape(bp, ngrp, -1))
                fi.append(f_im.reshape(bp, ngrp, -1))
            new_re.append(jnp.stack(fr, axis=1))
            new_im.append(jnp.stack(fi, axis=1))

            pad8 = lambda v: jnp.pad(v.reshape(1, -1).astype(F32), ((0, 0), (0, LANES - v.size)))
            n_slabs = 3 * db // LANES
            conv_w = gdn_conv_w[e].astype(F32).reshape(CONV_K, n_slabs, LANES).transpose(1, 0, 2)
            conv_b = gdn_conv_b[e].astype(F32).reshape(n_slabs, 1, LANES)
            os_, fd = [], []
            for dd in range(n_dir):
                s0 = _group_states(state_delta[:, e, dd], flatten=False)
                o_d, s_fin = _gdn_scan(lay, qkv, ab, conv_w, conv_b, pad8(gdn_a_log[e]),
                                       pad8(gdn_dt_bias[e]), s0, reverse=(dd == 1), direction=dd)
                os_.append(o_d)
                fd.append(s_fin.reshape((bp,) + s_fin.shape[2:]))
            new_delta.append(jnp.stack(fd, axis=1))

            acts = [u, za, ys[0], ys[1], os_[0], os_[1], zb]
            small = [row(s5_d[e]), row(gdn_o_norm[e]), row(norm_mix_post[l])]
            body, name, w_out = _tail_even_kernel, "tail_even", w_out_even[e]
        else:
            o = l // 2
            dr = lru_lam.shape[2]
            w_in = w_in_odd[o].astype(BF16)
            xb, yg = _pre_mixer(lay, x, mod_l, row(norm_mix_pre[l]), w_in,
                                ((0, dr, False), (dr, 2 * dr, False)))
            hs, fl = [], []
            for dd in range(n_dir):
                h, h_fin = _lru_scan(lay, xb, _group_states(state_lru[:, o, dd]),
                                     lru_conv_w[o].astype(F32), row(lru_conv_b[o]),
                                     lru_w_r[o, dd].astype(BF16), lru_w_i[o, dd].astype(BF16),
                                     row(lru_b_r[o, dd]), row(lru_b_i[o, dd]), row(lru_lam[o, dd]),
                                     reverse=(dd == 1))
                hs.append(h)
                fl.append(h_fin.reshape(bp, dr))
            new_lru.append(jnp.stack(fl, axis=1))
            acts = [hs[0], hs[1], yg]
            small = [row(norm_mix_post[l])]
            body, name, w_out = _tail_odd_kernel, "tail_odd", w_out_odd[o]

        x = _layer_tail(lay, body, name, acts, x, mod_l, small, w_out.astype(BF16),
                        row(norm_mlp_pre[l]), row(norm_mlp_post[l]),
                        w_mlp_in[l].astype(BF16), w_mlp_out[l].astype(BF16))

    y_prompt = _from_time_major(x[:lay.ctx_rows], bp, tp).astype(x_prompt.dtype)
    y_sample = _from_time_major(x[lay.ctx_rows:], bl, tl).astype(x_sample.dtype)
    p_a = s5_lam_re.shape[3]
    new_s5_re = jnp.stack(new_re, axis=1).reshape(bp, -1, n_dir, ngrp, p_a)
    new_s5_im = jnp.stack(new_im, axis=1).reshape(bp, -1, n_dir, ngrp, p_a)
    return (y_prompt, y_sample, new_s5_re, new_s5_im,
            jnp.stack(new_delta, axis=1), jnp.stack(new_lru, axis=1))
```

```python
import functools
from typing import NamedTuple

import jax
import jax.numpy as jnp
from jax import lax
from jax.experimental import pallas as pl
from jax.experimental.pallas import tpu as pltpu

F32 = jnp.float32
BF16 = jnp.bfloat16
HIGHEST = lax.Precision.HIGHEST

EPS = 1e-6
POS_BASE = 10000.0
GRID_W = 64
N_MOD = 6
S5_GROUP = 16
S5_STATES = 64
GDN_HEADS = 4
GDN_CHUNK = 64
CONV_K = 4
CONV_LEFT = (CONV_K - 1) // 2
LRU_BLOCKS = 4
LRU_C = 8.0

SUBLANES = 8
LANES = 128
SEQ_GROUP = SUBLANES
SCAN_STEPS = 32
SCAN_ROWS = SCAN_STEPS * SEQ_GROUP
GDN_TILE_ROWS = GDN_CHUNK * SEQ_GROUP
GDN_SEQ_BATCH = 8
RELAYOUT_STEPS = 64
TOKEN_ROWS = 512
MLP_HIDDEN_BLOCK = 2048
S5_SLICE = 128
VMEM_LIMIT_BYTES = 56 * 1024 * 1024


class Layout(NamedTuple):
    n_ctx: int
    len_ctx: int
    n_lat: int
    len_lat: int

    @property
    def ctx_rows(self):
        return self.n_ctx * self.len_ctx

    @property
    def rows(self):
        return self.ctx_rows + self.n_lat * self.len_lat

    @property
    def groups(self):
        return self.n_ctx + self.n_lat

    def tiles(self, tile_rows):
        return self.ctx_rows // tile_rows + self.n_lat * (self.len_lat // tile_rows)


def _schedule(lay, tile_rows, reverse, i):
    nc = lay.len_ctx // tile_rows
    nl = lay.len_lat // tile_rows
    n_ctx_tiles = lay.n_ctx * nc
    is_ctx = i < n_ctx_tiles
    j = jnp.maximum(i - n_ctx_tiles, 0)
    ic = jnp.minimum(i, n_ctx_tiles - 1)
    g = jnp.where(is_ctx, ic // nc, lay.n_ctx + j // nl)
    step = jnp.where(is_ctx, ic % nc, j % nl)
    n = jnp.where(is_ctx, nc, nl)
    k = (n - 1 - step) if reverse else step
    base = jnp.where(is_ctx, (ic // nc) * nc, n_ctx_tiles + (j // nl) * nl)
    return base + k, g, step == 0, step == n - 1, k, n


def _params(n_axes=1):
    return pltpu.CompilerParams(dimension_semantics=("arbitrary",) * n_axes,
                                vmem_limit_bytes=VMEM_LIMIT_BYTES)


def _dot(a, b):
    return jnp.dot(a, b, preferred_element_type=F32)


def _dot_hi(a, b):
    return jnp.dot(a, b, precision=HIGHEST, preferred_element_type=F32)


def _dot_nt(a, b):
    return lax.dot_general(a, b, (((1,), (1,)), ((), ())), preferred_element_type=F32)


def _dot_tn(a, b):
    return lax.dot_general(a, b, (((0,), (0,)), ((), ())), preferred_element_type=F32)


def _rms(x):
    return x * lax.rsqrt(jnp.mean(x * x, axis=-1, keepdims=True) + EPS)


def _per_group(rows_val, vec8):
    r, c = rows_val.shape
    return rows_val.reshape(r // SEQ_GROUP, SEQ_GROUP, c), vec8[None]


def _modulate(h, shift8, scale8):
    h3, sc = _per_group(h, scale8)
    out = h3 * (1.0 + sc) + shift8[None]
    return out.reshape(h.shape)


def _gated_residual(x, y, gate8):
    y3, g = _per_group(y, gate8)
    return x + (y3 * g).reshape(x.shape)


def _mod_chunk(mod_ref, idx, d):
    return mod_ref[0, :, idx * d:(idx + 1) * d]


def _softplus(x):
    return jnp.maximum(x, 0.0) + jnp.log1p(jnp.exp(-jnp.abs(x)))


def _sigmoid(x):
    return 0.5 * jnp.tanh(0.5 * x) + 0.5


def _ada_kernel(c_ref, w_ref, b_ref, o_ref):
    s = jax.nn.silu(c_ref[...]).astype(BF16)
    o_ref[0] = _dot(s, w_ref[0].astype(BF16)) + b_ref[0]


def _ada_vectors(cond, w_ada, b_ada):
    depth, d, n = w_ada.shape
    rows = cond.shape[0]
    tn = 512
    return pl.pallas_call(
        _ada_kernel, name="ada_vectors",
        grid=(depth, n // tn),
        in_specs=[pl.BlockSpec((rows, d), lambda l, j: (0, 0)),
                  pl.BlockSpec((1, d, tn), lambda l, j: (l, 0, j)),
                  pl.BlockSpec((1, 1, tn), lambda l, j: (l, 0, j))],
        out_specs=pl.BlockSpec((1, rows, tn), lambda l, j: (l, 0, j)),
        out_shape=jax.ShapeDtypeStruct((depth, rows, n), F32),
        compiler_params=_params(2),
    )(cond, w_ada, b_ada.reshape(depth, 1, n))


def _family_maps(lay):
    sched = functools.partial(_schedule, lay, RELAYOUT_STEPS * SEQ_GROUP, False)
    nc = lay.len_ctx // (RELAYOUT_STEPS * SEQ_GROUP)

    def ctx(i):
        _, g, _, _, k, _ = sched(i)
        on = g < lay.n_ctx
        return (jnp.minimum(g, lay.n_ctx - 1), jnp.where(on, k, nc - 1), 0)

    def lat(i):
        _, g, _, _, k, _ = sched(i)
        on = g >= lay.n_ctx
        return (jnp.maximum(g - lay.n_ctx, 0), jnp.where(on, k, 0), 0)

    def lat_steps(i):
        return lat(i)[1:]

    return ctx, lat, lat_steps


def _embed_kernel(xc_ref, xl_ref, pos_ref, o_ref, slab_ref, *, lay):
    n_ctx_tiles = lay.ctx_rows // (RELAYOUT_STEPS * SEQ_GROUP)
    n_slabs = slab_ref.shape[0]

    def relayout(x_ref, pos):
        for b in range(SEQ_GROUP):
            xb = x_ref[b] if pos is None else x_ref[b] + pos
            for j in range(n_slabs):
                slab_ref[j, pl.ds(b, RELAYOUT_STEPS, stride=SEQ_GROUP), :] = (
                    xb[:, j * LANES:(j + 1) * LANES])
        for j in range(n_slabs):
            o_ref[:, j * LANES:(j + 1) * LANES] = slab_ref[j]

    @pl.when(pl.program_id(0) < n_ctx_tiles)
    def _():
        relayout(xc_ref, None)

    @pl.when(pl.program_id(0) >= n_ctx_tiles)
    def _():
        relayout(xl_ref, pos_ref[...])


def _embed(lay, x_ctx, x_lat, table):
    d = x_ctx.shape[-1]
    rows_tile = RELAYOUT_STEPS * SEQ_GROUP
    ctx, lat, lat_steps = _family_maps(lay)
    block = (SEQ_GROUP, RELAYOUT_STEPS, d)
    return pl.pallas_call(
        functools.partial(_embed_kernel, lay=lay), name="embed",
        grid=(lay.tiles(rows_tile),),
        in_specs=[pl.BlockSpec(block, ctx), pl.BlockSpec(block, lat),
                  pl.BlockSpec((RELAYOUT_STEPS, d), lat_steps)],
        out_specs=pl.BlockSpec((rows_tile, d), lambda i: (i, 0)),
        out_shape=jax.ShapeDtypeStruct((lay.rows, d), F32),
        scratch_shapes=[pltpu.VMEM((d // LANES, rows_tile, LANES), F32)],
        compiler_params=_params(),
    )(x_ctx, x_lat, table)


def _unembed_kernel(x_ref, yc_ref, yl_ref, slab_ref, *, lay):
    n_ctx_tiles = lay.ctx_rows // (RELAYOUT_STEPS * SEQ_GROUP)
    n_slabs = slab_ref.shape[0]
    for j in range(n_slabs):
        slab_ref[j] = x_ref[:, j * LANES:(j + 1) * LANES]

    def relayout(y_ref):
        for b in range(SEQ_GROUP):
            for j in range(n_slabs):
                y_ref[b, :, j * LANES:(j + 1) * LANES] = (
                    slab_ref[j, pl.ds(b, RELAYOUT_STEPS, stride=SEQ_GROUP), :])

    @pl.when(pl.program_id(0) < n_ctx_tiles)
    def _():
        relayout(yc_ref)

    @pl.when(pl.program_id(0) >= n_ctx_tiles)
    def _():
        relayout(yl_ref)


def _unembed(lay, x, shape_ctx, shape_lat):
    d = x.shape[-1]
    rows_tile = RELAYOUT_STEPS * SEQ_GROUP
    ctx, lat, _ = _family_maps(lay)
    block = (SEQ_GROUP, RELAYOUT_STEPS, d)
    return pl.pallas_call(
        functools.partial(_unembed_kernel, lay=lay), name="unembed",
        grid=(lay.tiles(rows_tile),),
        in_specs=[pl.BlockSpec((rows_tile, d), lambda i: (i, 0))],
        out_specs=[pl.BlockSpec(block, ctx), pl.BlockSpec(block, lat)],
        out_shape=[jax.ShapeDtypeStruct(shape_ctx, F32), jax.ShapeDtypeStruct(shape_lat, F32)],
        scratch_shapes=[pltpu.VMEM((d // LANES, rows_tile, LANES), F32)],
        compiler_params=_params(),
    )(x)


def _grid_sincos_table(n_tokens, d_model):
    rows = n_tokens // GRID_W
    row = jnp.repeat(jnp.arange(rows, dtype=F32), GRID_W)
    col = jnp.tile(jnp.arange(GRID_W, dtype=F32), rows)
    n_freq = d_model // 4
    omega = POS_BASE ** (-jnp.arange(n_freq, dtype=F32) / n_freq)
    ar = row[:, None] * omega
    ac = col[:, None] * omega
    return jnp.concatenate([jnp.sin(ar), jnp.cos(ar), jnp.sin(ac), jnp.cos(ac)], axis=-1)


def _mod_index(lay, tile_rows):
    def index(i):
        row = i * tile_rows
        lat = jnp.maximum(row - lay.ctx_rows, 0) // lay.len_lat
        return jnp.where(row < lay.ctx_rows, 0, 1 + lat)
    return index


def _pre_kernel(x_ref, mod_ref, gain_ref, w_ref, *out_refs, splits):
    d = x_ref.shape[-1]
    h = _rms(x_ref[...]) * gain_ref[...]
    h = _modulate(h, _mod_chunk(mod_ref, 0, d), _mod_chunk(mod_ref, 1, d)).astype(BF16)
    for (a, b, slabs), o_ref in zip(splits, out_refs):
        if slabs:
            acc = _dot(h, w_ref[:, a:b])
            for j in range((b - a) // LANES):
                o_ref[j] = acc[:, j * LANES:(j + 1) * LANES]
        else:
            o_ref[...] = _dot(h, w_ref[:, a:b])


def _pre_mixer(lay, x, mod, gain, w, splits):
    rows, d = x.shape
    n = w.shape[1]
    midx = _mod_index(lay, TOKEN_ROWS)
    out_specs, out_shapes = [], []
    for a, b, slabs in splits:
        if slabs:
            ns = (b - a) // LANES
            out_specs.append(pl.BlockSpec((ns, TOKEN_ROWS, LANES), lambda i: (0, i, 0)))
            out_shapes.append(jax.ShapeDtypeStruct((ns, rows, LANES), F32))
        else:
            out_specs.append(pl.BlockSpec((TOKEN_ROWS, b - a), lambda i: (i, 0)))
            out_shapes.append(jax.ShapeDtypeStruct((rows, b - a), F32))
    return pl.pallas_call(
        functools.partial(_pre_kernel, splits=splits), name="pre_mixer",
        grid=(rows // TOKEN_ROWS,),
        in_specs=[pl.BlockSpec((TOKEN_ROWS, d), lambda i: (i, 0)),
                  pl.BlockSpec((1, SEQ_GROUP, N_MOD * d), lambda i: (midx(i), 0, 0)),
                  pl.BlockSpec((1, d), lambda i: (0, 0)),
                  pl.BlockSpec((d, n), lambda i: (0, 0))],
        out_specs=out_specs, out_shape=out_shapes,
        compiler_params=_params(),
    )(x, mod, gain, w)


def _finish_sublayer(x, out, gain, gate8):
    return _gated_residual(x, _rms(out) * gain, gate8)


def _mlp_sublayer(x, mod_ref, gpre_ref, gpost_ref, w1_ref, w2_ref):
    d = x.shape[-1]
    dff = w1_ref.shape[1]
    h = _rms(x) * gpre_ref[...]
    h = _modulate(h, _mod_chunk(mod_ref, 3, d), _mod_chunk(mod_ref, 4, d)).astype(BF16)
    out = None
    for a in range(0, dff, MLP_HIDDEN_BLOCK):
        f = jnp.square(jnp.maximum(_dot(h, w1_ref[:, a:a + MLP_HIDDEN_BLOCK]), 0.0))
        part = _dot(f.astype(BF16), w2_ref[a:a + MLP_HIDDEN_BLOCK, :])
        out = part if out is None else out + part
    return _finish_sublayer(x, out, gpost_ref[...], _mod_chunk(mod_ref, 5, d))


def _tail_odd_kernel(hf_ref, hb_ref, yg_ref, x_ref, mod_ref, gmix_ref, wout_ref,
                     gpre_ref, gpost_ref, w1_ref, w2_ref, o_ref):
    d = x_ref.shape[-1]
    y = (hf_ref[...] + hb_ref[...]) * jax.nn.gelu(yg_ref[...])
    out = _dot(y.astype(BF16), wout_ref[...])
    x = _finish_sublayer(x_ref[...], out, gmix_ref[...], _mod_chunk(mod_ref, 2, d))
    o_ref[...] = _mlp_sublayer(x, mod_ref, gpre_ref, gpost_ref, w1_ref, w2_ref)


def _tail_even_kernel(u_ref, za_ref, yf_ref, yb_ref, of_ref, ob_ref, zb_ref, x_ref, mod_ref,
                      dskip_ref, onorm_ref, gmix_ref, wout_ref,
                      gpre_ref, gpost_ref, w1_ref, w2_ref, o_ref):
    d = x_ref.shape[-1]
    da = u_ref.shape[-1]
    heads, _, dv = of_ref.shape
    ya = jax.nn.gelu(yf_ref[...] + yb_ref[...] + u_ref[...] * dskip_ref[...])
    ya = ya * jax.nn.sigmoid(za_ref[...])
    out = _dot(ya.astype(BF16), wout_ref[0:da, :])
    for h in range(heads):
        o = of_ref[h] + ob_ref[h]
        oh = _rms(o) * onorm_ref[...] * jax.nn.silu(zb_ref[:, h * dv:(h + 1) * dv])
        out = out + _dot(oh.astype(BF16), wout_ref[da + h * dv:da + (h + 1) * dv, :])
    x = _finish_sublayer(x_ref[...], out, gmix_ref[...], _mod_chunk(mod_ref, 2, d))
    o_ref[...] = _mlp_sublayer(x, mod_ref, gpre_ref, gpost_ref, w1_ref, w2_ref)


def _layer_tail(lay, body, name, acts, x, mod, small, wout, gpre, gpost, w1, w2):
    rows, d = x.shape
    midx = _mod_index(lay, TOKEN_ROWS)

    def act_spec(a):
        if a.ndim == 3:
            return pl.BlockSpec((a.shape[0], TOKEN_ROWS, a.shape[2]), lambda i: (0, i, 0))
        return pl.BlockSpec((TOKEN_ROWS, a.shape[1]), lambda i: (i, 0))

    const = lambda a: pl.BlockSpec(a.shape, lambda i: (0, 0))
    resident = lambda a: pl.BlockSpec(a.shape, lambda i: (0, 0), pipeline_mode=pl.Buffered(1))
    return pl.pallas_call(
        body, name=name,
        grid=(rows // TOKEN_ROWS,),
        in_specs=[act_spec(a) for a in acts] + [
            act_spec(x), pl.BlockSpec((1, SEQ_GROUP, N_MOD * d), lambda i: (midx(i), 0, 0))]
            + [const(s) for s in small] + [resident(wout), const(gpre), const(gpost),
                                           resident(w1), resident(w2)],
        out_specs=act_spec(x),
        out_shape=jax.ShapeDtypeStruct((rows, d), F32),
        compiler_params=_params(),
    )(*acts, x, mod, *small, wout, gpre, gpost, w1, w2)


def _s5_discretise_kernel(lr_ref, li_ref, ldt_ref, bre_ref, bim_ref,
                          ar_ref, ai_ref, bbre_ref, bbim_ref):
    lr, li = lr_ref[...], li_ref[...]
    dt = jnp.exp(ldt_ref[...])
    mag = jnp.exp(lr * dt)
    ar = mag * jnp.cos(li * dt)
    ai = mag * jnp.sin(li * dt)
    den = lr * lr + li * li
    fr = ((ar - 1.0) * lr + ai * li) / den
    fi = (ai * lr - (ar - 1.0) * li) / den
    ar_ref[...] = ar
    ai_ref[...] = ai
    bbre_ref[...] = fr * bre_ref[...] - fi * bim_ref[...]
    bbim_ref[...] = fr * bim_ref[...] + fi * bre_ref[...]


def _s5_discretise(lam_re, lam_im, log_dt, b_re, b_im):
    e, nd, g, p = lam_re.shape
    c = b_re.shape[-1]
    full = (e, nd, g, c, p)
    flat = lambda a: jnp.broadcast_to(a, full).reshape(-1, p)
    lr = flat(lam_re[:, :, :, None, :])
    li = flat(lam_im[:, :, :, None, :])
    ldt = flat(log_dt[:, :, :, None, None])
    bre = flat(b_re.transpose(0, 1, 3, 2)[:, None])
    bim = flat(b_im.transpose(0, 1, 3, 2)[:, None])
    shape = jax.ShapeDtypeStruct(lr.shape, F32)
    ar, ai, bbre, bbim = pl.pallas_call(
        _s5_discretise_kernel, name="s5_discretise", out_shape=[shape] * 4,
        compiler_params=pltpu.CompilerParams(vmem_limit_bytes=VMEM_LIMIT_BYTES),
    )(lr, li, ldt, bre, bim)
    ar = ar.reshape(full)[:, :, :, 0, :]
    ai = ai.reshape(full)[:, :, :, 0, :]
    return ar, ai, bbre.reshape(full), bbim.reshape(full)


def _block_diag_in(bb):
    g, c, p = bb.shape
    per = S5_SLICE // c
    n = g // per
    blocks = bb.reshape(n, per, c, p)
    eye = jnp.eye(per, dtype=bb.dtype)
    return jnp.einsum('nicp,ij->nicjp', blocks, eye).reshape(n, per * c, per * p)


def _block_diag_out(cc):
    g, c, p = cc.shape
    per = S5_SLICE // c
    n = g // per
    blocks = cc.reshape(n, per, c, p)
    eye = jnp.eye(per, dtype=cc.dtype)
    return jnp.einsum('nicp,ij->nipjc', blocks, eye).reshape(n, per * p, per * c)


def _s5_scan_kernel(u_ref, h0re_ref, h0im_ref, a_ref, bw_ref, cw_ref,
                    y_ref, fre_ref, fim_ref, hre_ref, him_ref, bre_ref, bim_ref,
                    *, lay, reverse):
    _, grp, first, last, _, _ = _schedule(lay, SCAN_ROWS, reverse, pl.program_id(0))
    is_ctx = grp < lay.n_ctx
    n_slices = bw_ref.shape[0]
    wide = bw_ref.shape[2] // 2

    @pl.when(first)
    def _():
        hre_ref[...] = jnp.where(is_ctx, 0.0, h0re_ref[0])
        him_ref[...] = jnp.where(is_ctx, 0.0, h0im_ref[0])

    for j in range(n_slices):
        uj = u_ref[:, j * S5_SLICE:(j + 1) * S5_SLICE].astype(BF16)
        bb = _dot(uj, bw_ref[j])
        bre_ref[:, j * wide:(j + 1) * wide] = bb[:, :wide]
        bim_ref[:, j * wide:(j + 1) * wide] = bb[:, wide:]

    steps = range(SCAN_STEPS - 1, -1, -1) if reverse else range(SCAN_STEPS)
    for j in range(n_slices):
        cols = slice(j * wide, (j + 1) * wide)
        ar = a_ref[0, :, cols]
        ai = a_ref[1, :, cols]
        hr = hre_ref[:, cols]
        hi = him_ref[:, cols]
        for t in steps:
            rows = slice(t * SEQ_GROUP, (t + 1) * SEQ_GROUP)
            nr = ar * hr - ai * hi + bre_ref[rows, cols]
            ni = ar * hi + ai * hr + bim_ref[rows, cols]
            bre_ref[rows, cols] = nr
            bim_ref[rows, cols] = ni
            hr, hi = nr, ni
        hre_ref[:, cols] = hr
        him_ref[:, cols] = hi

    for j in range(n_slices):
        cols = slice(j * wide, (j + 1) * wide)
        yj = _dot(bre_ref[:, cols].astype(BF16), cw_ref[0, j])
        yj = yj - _dot(bim_ref[:, cols].astype(BF16), cw_ref[1, j])
        y_ref[:, j * S5_SLICE:(j + 1) * S5_SLICE] = yj

    @pl.when(last & is_ctx)
    def _():
        fre_ref[0] = hre_ref[...]
        fim_ref[0] = him_ref[...]


def _state_maps(lay, sched, ndim):
    pad = (0,) * (ndim - 1)
    start = lambda i: (jnp.maximum(sched(i)[1] - lay.n_ctx, 0),) + pad
    final = lambda i: (jnp.minimum(sched(i)[1], lay.n_ctx - 1),) + pad
    return start, final


def _s5_scan(lay, u, h0re, h0im, a8, bw, cw, reverse):
    rows, da = u.shape
    ns = h0re.shape[-1]
    sched = functools.partial(_schedule, lay, SCAN_ROWS, reverse)
    tile = lambda i: (sched(i)[0], 0)
    start, final = _state_maps(lay, sched, 3)
    state_shape = jax.ShapeDtypeStruct((lay.n_ctx, SEQ_GROUP, ns), F32)
    return pl.pallas_call(
        functools.partial(_s5_scan_kernel, lay=lay, reverse=reverse), name="s5_scan",
        grid=(lay.tiles(SCAN_ROWS),),
        in_specs=[pl.BlockSpec((SCAN_ROWS, da), tile),
                  pl.BlockSpec((1, SEQ_GROUP, ns), start),
                  pl.BlockSpec((1, SEQ_GROUP, ns), start),
                  pl.BlockSpec(a8.shape, lambda i: (0, 0, 0)),
                  pl.BlockSpec(bw.shape, lambda i: (0, 0, 0)),
                  pl.BlockSpec(cw.shape, lambda i: (0, 0, 0, 0))],
        out_specs=[pl.BlockSpec((SCAN_ROWS, da), tile),
                   pl.BlockSpec((1, SEQ_GROUP, ns), final),
                   pl.BlockSpec((1, SEQ_GROUP, ns), final)],
        out_shape=[jax.ShapeDtypeStruct((rows, da), F32), state_shape, state_shape],
        scratch_shapes=[pltpu.VMEM((SEQ_GROUP, ns), F32), pltpu.VMEM((SEQ_GROUP, ns), F32),
                        pltpu.VMEM((SCAN_ROWS, ns), F32), pltpu.VMEM((SCAN_ROWS, ns), F32)],
        compiler_params=_params(),
    )(u, h0re, h0im, a8, bw, cw)


def _halo_specs(lay, steps, channels, reverse=False, slabs=None):
    sched = functools.partial(_schedule, lay, steps * SEQ_GROUP, reverse)
    last_pair = lay.rows // (2 * SEQ_GROUP) - 1
    lead = () if slabs is None else (0,)
    shape = () if slabs is None else (slabs,)

    def prev(i):
        t = sched(i)[0]
        return lead + (jnp.maximum(t * steps - 1, 0), 0)

    def nxt(i):
        t = sched(i)[0]
        return lead + (jnp.minimum((t + 1) * (steps // 2), last_pair), 0)

    return (pl.BlockSpec(shape + (SEQ_GROUP, channels), prev),
            pl.BlockSpec(shape + (2 * SEQ_GROUP, channels), nxt))


def _conv_centred(x, prev, nxt, k, n, w, b):
    prev = jnp.where(k > 0, prev, 0.0)
    nxt = jnp.where(k < n - 1, nxt, 0.0)
    xp = jnp.concatenate([prev, x, nxt], axis=0)
    r = x.shape[0]
    out = b + xp[0:r] * w[0:1, :]
    for j in range(1, CONV_K):
        out = out + xp[j * SEQ_GROUP:j * SEQ_GROUP + r] * w[j:j + 1, :]
    return out


def _lru_scan_kernel(x_ref, xprev_ref, xnext_ref, h0_ref, cw_ref, cb_ref, wr_ref, wi_ref,
                     br_ref, bi_ref, lam_ref, h_ref, fin_ref, st_ref, a_ref, b_ref,
                     *, lay, reverse):
    _, grp, first, last, k, n = _schedule(lay, SCAN_ROWS, reverse, pl.program_id(0))
    is_ctx = grp < lay.n_ctx

    @pl.when(first)
    def _():
        st_ref[...] = jnp.where(is_ctx, 0.0, h0_ref[0])

    x = _conv_centred(x_ref[...], xprev_ref[...], xnext_ref[...], k, n, cw_ref[...], cb_ref[...])
    xb = x.astype(BF16)
    bs = wr_ref.shape[-1]
    neg_sp = -LRU_C * _softplus(-lam_ref[...])
    for blk in range(LRU_BLOCKS):
        cols = slice(blk * bs, (blk + 1) * bs)
        r = _sigmoid(_dot(xb[:, cols], wr_ref[blk]) + br_ref[:, cols])
        gi = _sigmoid(_dot(xb[:, cols], wi_ref[blk]) + bi_ref[:, cols])
        log_a = neg_sp[:, cols] * r
        th = jnp.tanh(log_a)
        a_ref[:, cols] = jnp.exp(log_a)
        b_ref[:, cols] = jnp.sqrt(-2.0 * th / (1.0 - th)) * (gi * x[:, cols])

    steps = range(SCAN_STEPS - 1, -1, -1) if reverse else range(SCAN_STEPS)
    h = st_ref[...]
    for t in steps:
        rows = slice(t * SEQ_GROUP, (t + 1) * SEQ_GROUP)
        h = a_ref[rows, :] * h + b_ref[rows, :]
        h_ref[rows, :] = h
    st_ref[...] = h

    @pl.when(last & is_ctx)
    def _():
        fin_ref[0] = st_ref[...]


def _lru_scan(lay, xb, h0, conv_w, conv_b, wr, wi, br, bi, lam, reverse):
    rows, d = xb.shape
    sched = functools.partial(_schedule, lay, SCAN_ROWS, reverse)
    tile = lambda i: (sched(i)[0], 0)
    start, final = _state_maps(lay, sched, 3)
    const2 = lambda a: pl.BlockSpec(a.shape, lambda i: (0, 0))
    const3 = lambda a: pl.BlockSpec(a.shape, lambda i: (0, 0, 0))
    prev_spec, next_spec = _halo_specs(lay, SCAN_STEPS, d, reverse)
    return pl.pallas_call(
        functools.partial(_lru_scan_kernel, lay=lay, reverse=reverse), name="lru_scan",
        grid=(lay.tiles(SCAN_ROWS),),
        in_specs=[pl.BlockSpec((SCAN_ROWS, d), tile), prev_spec, next_spec,
                  pl.BlockSpec((1, SEQ_GROUP, d), start),
                  const2(conv_w), const2(conv_b), const3(wr), const3(wi),
                  const2(br), const2(bi), const2(lam)],
        out_specs=[pl.BlockSpec((SCAN_ROWS, d), tile), pl.BlockSpec((1, SEQ_GROUP, d), final)],
        out_shape=[jax.ShapeDtypeStruct((rows, d), F32),
                   jax.ShapeDtypeStruct((lay.n_ctx, SEQ_GROUP, d), F32)],
        scratch_shapes=[pltpu.VMEM((SEQ_GROUP, d), F32),
                        pltpu.VMEM((SCAN_ROWS, d), F32), pltpu.VMEM((SCAN_ROWS, d), F32)],
        compiler_params=_params(),
    )(xb, xb, xb, h0, conv_w, conv_b, wr, wi, br, bi, lam)


def _unit_triangular_inverses(lmats, eye, blocks):
    c = lmats[0].shape[0]
    ts = [jnp.where(blocks[0], -l, 0.0) for l in lmats]
    tbs = [t.astype(BF16) for t in ts]
    invs = [eye + t for t in ts]
    ps = [_dot(tb, tb) for tb in tbs]
    ys = [_dot(jnp.concatenate([x, p], axis=0).astype(BF16), p.astype(BF16))
          for x, p in zip(invs, ps)]
    invs = [x + y[:c] for x, y in zip(invs, ys)]
    zs = [_dot(x.astype(BF16), y[c:].astype(BF16)) for x, y in zip(invs, ys)]
    invs = [x + z for x, z in zip(invs, zs)]
    inside = blocks[0]
    for outer in blocks[1:] + [None]:
        off = ~inside if outer is None else (outer & ~inside)
        es = [jnp.where(off, l, 0.0).astype(BF16) for l in lmats]
        xbs = [x.astype(BF16) for x in invs]
        xes = [_dot(xb, e).astype(BF16) for xb, e in zip(xbs, es)]
        xexs = [_dot(xe, xb) for xe, xb in zip(xes, xbs)]
        invs = [x - xex for x, xex in zip(invs, xexs)]
        inside = outer
    return invs


def _gdn_prep_kernel(x_ref, xprev_ref, xnext_ref, ab_ref, cw_ref, cb_ref, alog_ref, dtb_ref,
                     s0_ref, o_ref, fin_ref, qkv_ref, gate_ref, s_ref, *, lay, reverse, direction):
    _, _, _, _, kpos, npos = _schedule(lay, GDN_TILE_ROWS, reverse, pl.program_id(0))
    dk = x_ref.shape[-1]
    for j in range(3 * GDN_HEADS):
        x = _conv_centred(x_ref[j], xprev_ref[j], xnext_ref[j], kpos, npos, cw_ref[j], cb_ref[j])
        x = jax.nn.silu(x)
        if j < 2 * GDN_HEADS:
            x = x * lax.rsqrt(jnp.sum(x * x, axis=-1, keepdims=True) + EPS)
        if j < GDN_HEADS:
            x = x * (dk ** -0.5)
        qkv_ref[j] = x
    ab = ab_ref[...]
    g_all = -jnp.exp(alog_ref[...]) * _softplus(ab + dtb_ref[...])
    lane_id = lax.broadcasted_iota(jnp.int32, ab.shape, 1)
    gate_ref[...] = jnp.where(lane_id < 2 * GDN_HEADS, g_all, jax.nn.sigmoid(ab))
    _gdn_kernel(qkv_ref, gate_ref, s0_ref, o_ref, fin_ref, s_ref,
                lay=lay, reverse=reverse, direction=direction)


def _gdn_kernel(qkv_ref, gate_ref, s0_ref, o_ref, fin_ref, s_ref, *, lay, reverse, direction):
    _, grp, first, last, _, _ = _schedule(lay, GDN_TILE_ROWS, reverse, pl.program_id(0))
    is_ctx = grp < lay.n_ctx
    c = GDN_CHUNK

    @pl.when(first)
    def _():
        s_ref[...] = jnp.where(is_ctx, 0.0, s0_ref[0])

    ri = lax.broadcasted_iota(jnp.int32, (c, c), 0)
    ci = lax.broadcasted_iota(jnp.int32, (c, c), 1)
    incl = (ri <= ci) if reverse else (ri >= ci)
    strict = (ri < ci) if reverse else (ri > ci)
    tri = jnp.where(incl, 1.0, 0.0).astype(F32)
    eye = jnp.where(ri == ci, 1.0, 0.0).astype(F32)
    blocks = []
    shift = 3
    while (1 << shift) < c:
        blocks.append((ri >> shift) == (ci >> shift))
        shift += 1
    end_row = 0 if reverse else c - 1
    seq_rows = lambda b: pl.ds(b, c, stride=SEQ_GROUP)
    for b0 in range(0, SEQ_GROUP, GDN_SEQ_BATCH):
        _delta_rule_chunk(range(b0, b0 + GDN_SEQ_BATCH), qkv_ref, gate_ref, s_ref, o_ref, seq_rows,
                          (incl, strict, tri, eye, blocks), end_row, direction)

    @pl.when(last & is_ctx)
    def _():
        fin_ref[0] = s_ref[...]


def _delta_rule_chunk(seqs, qkv_ref, gate_ref, s_ref, o_ref, seq_rows, masks, end_row, direction):
    incl, strict, tri, eye, blocks = masks
    c = GDN_CHUNK
    dk = qkv_ref.shape[-1]
    keys = [(b, h) for b in seqs for h in range(GDN_HEADS)]

    gcols, grams, decays, kbs, qes, betas = {}, {}, {}, {}, {}, {}
    for b in seqs:
        gates = gate_ref[seq_rows(b), :]
        gc = _dot_hi(tri, gates)
        gc_t = gc.T
        for h in range(GDN_HEADS):
            lane = direction * GDN_HEADS + h
            q = qkv_ref[h, seq_rows(b), :]
            k = qkv_ref[GDN_HEADS + h, seq_rows(b), :]
            beta = gates[:, 2 * GDN_HEADS + lane:2 * GDN_HEADS + lane + 1]
            gcol = gc[:, lane:lane + 1]
            grow = gc_t[lane:lane + 1, :]
            egc = jnp.exp(gcol)
            gcols[b, h] = gcol
            betas[b, h] = beta
            decays[b, h] = jnp.exp(jnp.where(incl, gcol - grow, -1e30))
            kb = k * beta
            kbs[b, h] = kb * egc
            qes[b, h] = q * egc
            grams[b, h] = _dot_nt(jnp.concatenate([kb, q], axis=0).astype(BF16), k.astype(BF16))
    lmats = [jnp.where(strict, grams[key][:c] * decays[key], 0.0) for key in keys]
    invs = _unit_triangular_inverses(lmats, eye, blocks)
    uws = {}
    for key, inv in zip(keys, invs):
        b, h = key
        vb = qkv_ref[2 * GDN_HEADS + h, seq_rows(b), :] * betas[key]
        rhs = jnp.concatenate([vb, kbs[key]], axis=1)
        uws[key] = _dot(inv.astype(BF16), rhs.astype(BF16))

    ss = {key: s_ref[key[0], key[1]] for key in keys}
    projs = {key: _dot(jnp.concatenate([uws[key][:, dk:], qes[key]], axis=0).astype(BF16),
                       ss[key].astype(BF16)) for key in keys}
    v_news = {key: uws[key][:, :dk] - projs[key][:c] for key in keys}
    for key in keys:
        b, h = key
        k = qkv_ref[GDN_HEADS + h, seq_rows(b), :]
        g_end = gcols[key][end_row:end_row + 1, :]
        kd = k * jnp.exp(g_end - gcols[key])
        s_ref[b, h] = ss[key] * jnp.exp(g_end) + _dot_tn(kd.astype(BF16),
                                                         v_news[key].astype(BF16))
    for key in keys:
        b, h = key
        qk = jnp.where(incl, grams[key][c:] * decays[key], 0.0)
        o_ref[h, seq_rows(b), :] = projs[key][c:] + _dot(qk.astype(BF16),
                                                         v_news[key].astype(BF16))


def _gdn_scan(lay, qkv, gates, s0, reverse, direction, prep=None):
    slabs, rows, dk = qkv.shape
    sched = functools.partial(_schedule, lay, GDN_TILE_ROWS, reverse)
    tile3 = lambda i: (0, sched(i)[0], 0)
    tile2 = lambda i: (sched(i)[0], 0)
    start, final = _state_maps(lay, sched, 5)
    sblock = (1,) + s0.shape[1:]
    qkv_spec = pl.BlockSpec((slabs, GDN_TILE_ROWS, dk), tile3)
    gate_spec = pl.BlockSpec((GDN_TILE_ROWS, LANES), tile2)
    out_specs = [pl.BlockSpec((GDN_HEADS, GDN_TILE_ROWS, dk), tile3), pl.BlockSpec(sblock, final)]
    out_shape = [jax.ShapeDtypeStruct((GDN_HEADS, rows, dk), F32),
                 jax.ShapeDtypeStruct((lay.n_ctx,) + s0.shape[1:], F32)]
    static = dict(lay=lay, reverse=reverse, direction=direction)
    if prep is None:
        body, name = functools.partial(_gdn_kernel, **static), "gdn_scan"
        in_specs = [qkv_spec, gate_spec, pl.BlockSpec(sblock, start)]
        args = (qkv, gates, s0)
    else:
        body, name = functools.partial(_gdn_prep_kernel, **static), "gdn_prep_scan"
        const2 = lambda a: pl.BlockSpec(a.shape, lambda i: (0, 0))
        const3 = lambda a: pl.BlockSpec(a.shape, lambda i: (0, 0, 0))
        prev_spec, next_spec = _halo_specs(lay, GDN_CHUNK, dk, reverse, slabs=slabs)
        conv_w, conv_b, alog, dtb = prep
        in_specs = [qkv_spec, prev_spec, next_spec, gate_spec, const3(conv_w), const3(conv_b),
                    const2(alog), const2(dtb), pl.BlockSpec(sblock, start)]
        args = (qkv, qkv, qkv, gates, conv_w, conv_b, alog, dtb, s0)
        out_specs += [qkv_spec, gate_spec]
        out_shape += [jax.ShapeDtypeStruct(qkv.shape, F32), jax.ShapeDtypeStruct(gates.shape, F32)]
    return pl.pallas_call(
        body, name=name, grid=(lay.tiles(GDN_TILE_ROWS),),
        in_specs=in_specs, out_specs=out_specs, out_shape=out_shape,
        scratch_shapes=[pltpu.VMEM(s0.shape[1:], F32)],
        compiler_params=_params(),
    )(*args)


def _group_states(lat_state, flatten=True):
    db = lat_state.shape[0]
    tail = (-1,) if flatten else lat_state.shape[1:]
    return lat_state.reshape((db // SEQ_GROUP, SEQ_GROUP) + tail).astype(F32)


def kernel(x_prompt, x_sample, state_s5_re, state_s5_im, state_delta, state_lru, c, c_ctx, w_ada, b_ada, norm_mix_pre, norm_mix_post, norm_mlp_pre, norm_mlp_post, w_mlp_in, w_mlp_out, w_in_even, w_out_even, s5_lam_re, s5_lam_im, s5_log_dt, s5_b_re, s5_b_im, s5_c_re, s5_c_im, s5_d, gdn_conv_w, gdn_conv_b, gdn_a_log, gdn_dt_bias, gdn_o_norm, w_in_odd, w_out_odd, lru_conv_w, lru_conv_b, lru_w_r, lru_b_r, lru_w_i, lru_b_i, lru_lam):
    bp, tp, d = x_prompt.shape
    bl, tl, _ = x_sample.shape
    depth = w_ada.shape[0]
    n_dir = 2
    da = s5_d.shape[1]
    db = gdn_conv_w.shape[2] // 3
    heads = GDN_HEADS
    ngrp = s5_lam_re.shape[2]

    lay = Layout(bp // SEQ_GROUP, tp * SEQ_GROUP, bl // SEQ_GROUP, tl * SEQ_GROUP)

    cond = jnp.concatenate([c_ctx[None].astype(F32), c.astype(F32)], axis=0)
    pad = (-cond.shape[0]) % SUBLANES
    cond = jnp.pad(cond, ((0, pad), (0, 0)))
    ada = _ada_vectors(cond, w_ada, b_ada)
    mod_ctx = jnp.broadcast_to(ada[:, 0:1], (depth, SEQ_GROUP, ada.shape[-1]))[:, None]
    mod_lat = ada[:, 1:1 + bl].reshape(depth, bl // SEQ_GROUP, SEQ_GROUP, -1)
    mod = jnp.concatenate([mod_ctx, mod_lat], axis=1)

    x = _embed(lay, x_prompt.astype(F32), x_sample.astype(F32), _grid_sincos_table(tl, d))

    a_re, a_im, bb_re, bb_im = _s5_discretise(s5_lam_re, s5_lam_im, s5_log_dt, s5_b_re, s5_b_im)
    c_re_t = s5_c_re
    c_im_t = s5_c_im

    row = lambda v: v.reshape(1, -1).astype(F32)
    new_re, new_im, new_delta, new_lru = [], [], [], []
    for l in range(depth):
        mod_l = mod[l]
        if l % 2 == 0:
            e = l // 2
            n_in = w_in_even.shape[2]
            n_pad = (-n_in) % LANES
            w_in = jnp.pad(w_in_even[e], ((0, 0), (0, n_pad))).astype(BF16)
            splits = ((0, da, False), (da, 2 * da, False), (2 * da, 2 * da + 3 * db, True),
                      (2 * da + 3 * db, 2 * da + 4 * db, False),
                      (2 * da + 4 * db, n_in + n_pad, False))
            u, za, qkv, zb, ab = _pre_mixer(lay, x, mod_l, row(norm_mix_pre[l]), w_in, splits)

            cw = jnp.stack([_block_diag_out(c_re_t[e]), _block_diag_out(c_im_t[e])]).astype(BF16)
            ys, fr, fi = [], [], []
            for dd in range(n_dir):
                a8 = jnp.stack([a_re[e, dd].reshape(-1), a_im[e, dd].reshape(-1)])
                a8 = jnp.broadcast_to(a8[:, None, :], (2, SEQ_GROUP, a8.shape[-1]))
                bw = jnp.concatenate([_block_diag_in(bb_re[e, dd]), _block_diag_in(bb_im[e, dd])],
                                     axis=-1).astype(BF16)
                y, f_re, f_im = _s5_scan(lay, u, _group_states(state_s5_re[:, e, dd]),
                                         _group_states(state_s5_im[:, e, dd]),
                                         a8, bw, cw, reverse=(dd == 1))
                ys.append(y)
                fr.append(f_re.reshape(bp, ngrp, -1))
                fi.append(f_im.reshape(bp, ngrp, -1))
            new_re.append(jnp.stack(fr, axis=1))
            new_im.append(jnp.stack(fi, axis=1))

            pad8 = lambda v: jnp.pad(v.reshape(1, -1).astype(F32), ((0, 0), (0, LANES - v.size)))
            n_slabs = 3 * db // LANES
            conv_w = gdn_conv_w[e].astype(F32).reshape(CONV_K, n_slabs, LANES).transpose(1, 0, 2)
            conv_b = gdn_conv_b[e].astype(F32).reshape(n_slabs, 1, LANES)
            os_, fd = [], []
            prep = (conv_w, conv_b, pad8(gdn_a_log[e]), pad8(gdn_dt_bias[e]))
            for dd in range(n_dir):
                s0 = _group_states(state_delta[:, e, dd], flatten=False)
                if dd == 0:
                    o_d, s_fin, qkv, ab = _gdn_scan(lay, qkv, ab, s0, False, dd, prep=prep)
                else:
                    o_d, s_fin = _gdn_scan(lay, qkv, ab, s0, True, dd)
                os_.append(o_d)
                fd.append(s_fin.reshape((bp,) + s_fin.shape[2:]))
            new_delta.append(jnp.stack(fd, axis=1))

            acts = [u, za, ys[0], ys[1], os_[0], os_[1], zb]
            small = [row(s5_d[e]), row(gdn_o_norm[e]), row(norm_mix_post[l])]
            body, name, w_out = _tail_even_kernel, "tail_even", w_out_even[e]
        else:
            o = l // 2
            dr = lru_lam.shape[2]
            w_in = w_in_odd[o].astype(BF16)
            xb, yg = _pre_mixer(lay, x, mod_l, row(norm_mix_pre[l]), w_in,
                                ((0, dr, False), (dr, 2 * dr, False)))
            hs, fl = [], []
            for dd in range(n_dir):
                h, h_fin = _lru_scan(lay, xb, _group_states(state_lru[:, o, dd]),
                                     lru_conv_w[o].astype(F32), row(lru_conv_b[o]),
                                     lru_w_r[o, dd].astype(BF16), lru_w_i[o, dd].astype(BF16),
                                     row(lru_b_r[o, dd]), row(lru_b_i[o, dd]), row(lru_lam[o, dd]),
                                     reverse=(dd == 1))
                hs.append(h)
                fl.append(h_fin.reshape(bp, dr))
            new_lru.append(jnp.stack(fl, axis=1))
            acts = [hs[0], hs[1], yg]
            small = [row(norm_mix_post[l])]
            body, name, w_out = _tail_odd_kernel, "tail_odd", w_out_odd[o]

        x = _layer_tail(lay, body, name, acts, x, mod_l, small, w_out.astype(BF16),
                        row(norm_mlp_pre[l]), row(norm_mlp_post[l]),
                        w_mlp_in[l].astype(BF16), w_mlp_out[l].astype(BF16))

    y_prompt, y_sample = _unembed(lay, x, x_prompt.shape, x_sample.shape)
    y_prompt = y_prompt.astype(x_prompt.dtype)
    y_sample = y_sample.astype(x_sample.dtype)
    p_a = s5_lam_re.shape[3]
    new_s5_re = jnp.stack(new_re, axis=1).reshape(bp, -1, n_dir, ngrp, p_a)
    new_s5_im = jnp.stack(new_im, axis=1).reshape(bp, -1, n_dir, ngrp, p_a)
    return (y_prompt, y_sample, new_s5_re, new_s5_im,
            jnp.stack(new_delta, axis=1), jnp.stack(new_lru, axis=1))
```

```python
import functools
from typing import NamedTuple

import jax
import jax.numpy as jnp
from jax import lax
from jax.experimental import pallas as pl
from jax.experimental.pallas import tpu as pltpu

F32 = jnp.float32
BF16 = jnp.bfloat16
HIGHEST = lax.Precision.HIGHEST

EPS = 1e-6
POS_BASE = 10000.0
GRID_W = 64
N_MOD = 6
S5_GROUP = 16
S5_STATES = 64
GDN_HEADS = 4
GDN_CHUNK = 64
CONV_K = 4
CONV_LEFT = (CONV_K - 1) // 2
LRU_BLOCKS = 4
LRU_C = 8.0

SUBLANES = 8
LANES = 128
SEQ_GROUP = SUBLANES
SCAN_STEPS = 32
SCAN_ROWS = SCAN_STEPS * SEQ_GROUP
GDN_TILE_ROWS = GDN_CHUNK * SEQ_GROUP
GDN_SEQ_BATCH = 8
RELAYOUT_STEPS = 64
TOKEN_ROWS = 512
PRE_ROWS = 1024
MLP_HIDDEN_BLOCK = 2048
TAIL_ROW_BLOCK = 512
S5_SLICE = 128
VMEM_LIMIT_BYTES = 56 * 1024 * 1024


class Layout(NamedTuple):
    n_ctx: int
    len_ctx: int
    n_lat: int
    len_lat: int

    @property
    def ctx_rows(self):
        return self.n_ctx * self.len_ctx

    @property
    def rows(self):
        return self.ctx_rows + self.n_lat * self.len_lat

    @property
    def groups(self):
        return self.n_ctx + self.n_lat

    def tiles(self, tile_rows):
        return self.ctx_rows // tile_rows + self.n_lat * (self.len_lat // tile_rows)


def _schedule(lay, tile_rows, reverse, i):
    nc = lay.len_ctx // tile_rows
    nl = lay.len_lat // tile_rows
    n_ctx_tiles = lay.n_ctx * nc
    is_ctx = i < n_ctx_tiles
    j = jnp.maximum(i - n_ctx_tiles, 0)
    ic = jnp.minimum(i, n_ctx_tiles - 1)
    g = jnp.where(is_ctx, ic // nc, lay.n_ctx + j // nl)
    step = jnp.where(is_ctx, ic % nc, j % nl)
    n = jnp.where(is_ctx, nc, nl)
    k = (n - 1 - step) if reverse else step
    base = jnp.where(is_ctx, (ic // nc) * nc, n_ctx_tiles + (j // nl) * nl)
    return base + k, g, step == 0, step == n - 1, k, n


def _params(n_axes=1):
    return pltpu.CompilerParams(dimension_semantics=("arbitrary",) * n_axes,
                                vmem_limit_bytes=VMEM_LIMIT_BYTES)


def _dot(a, b):
    return jnp.dot(a, b, preferred_element_type=F32)


def _dot_hi(a, b):
    return jnp.dot(a, b, precision=HIGHEST, preferred_element_type=F32)


def _dot_nt(a, b):
    return lax.dot_general(a, b, (((1,), (1,)), ((), ())), preferred_element_type=F32)


def _dot_tn(a, b):
    return lax.dot_general(a, b, (((0,), (0,)), ((), ())), preferred_element_type=F32)


def _rms(x):
    return x * lax.rsqrt(jnp.mean(x * x, axis=-1, keepdims=True) + EPS)


def _per_group(rows_val, vec8):
    r, c = rows_val.shape
    return rows_val.reshape(r // SEQ_GROUP, SEQ_GROUP, c), vec8[None]


def _modulate(h, shift8, scale8):
    h3, sc = _per_group(h, scale8)
    out = h3 * (1.0 + sc) + shift8[None]
    return out.reshape(h.shape)


def _gated_residual(x, y, gate8):
    y3, g = _per_group(y, gate8)
    return x + (y3 * g).reshape(x.shape)


def _mod_chunk(mod_ref, idx, d):
    return mod_ref[0, :, idx * d:(idx + 1) * d]


def _softplus(x):
    return jnp.maximum(x, 0.0) + jnp.log1p(jnp.exp(-jnp.abs(x)))


def _sigmoid(x):
    return 0.5 * jnp.tanh(0.5 * x) + 0.5


def _ada_kernel(c_ref, w_ref, b_ref, o_ref):
    s = jax.nn.silu(c_ref[...]).astype(BF16)
    o_ref[0] = _dot(s, w_ref[0].astype(BF16)) + b_ref[0]


def _ada_vectors(cond, w_ada, b_ada):
    depth, d, n = w_ada.shape
    rows = cond.shape[0]
    tn = 512
    return pl.pallas_call(
        _ada_kernel, name="ada_vectors",
        grid=(depth, n // tn),
        in_specs=[pl.BlockSpec((rows, d), lambda l, j: (0, 0)),
                  pl.BlockSpec((1, d, tn), lambda l, j: (l, 0, j)),
                  pl.BlockSpec((1, 1, tn), lambda l, j: (l, 0, j))],
        out_specs=pl.BlockSpec((1, rows, tn), lambda l, j: (l, 0, j)),
        out_shape=jax.ShapeDtypeStruct((depth, rows, n), F32),
        compiler_params=_params(2),
    )(cond, w_ada, b_ada.reshape(depth, 1, n))


def _family_maps(lay):
    sched = functools.partial(_schedule, lay, RELAYOUT_STEPS * SEQ_GROUP, False)
    nc = lay.len_ctx // (RELAYOUT_STEPS * SEQ_GROUP)

    def ctx(i):
        _, g, _, _, k, _ = sched(i)
        on = g < lay.n_ctx
        return (jnp.minimum(g, lay.n_ctx - 1), jnp.where(on, k, nc - 1), 0)

    def lat(i):
        _, g, _, _, k, _ = sched(i)
        on = g >= lay.n_ctx
        return (jnp.maximum(g - lay.n_ctx, 0), jnp.where(on, k, 0), 0)

    def lat_steps(i):
        return lat(i)[1:]

    return ctx, lat, lat_steps


def _embed_kernel(xc_ref, xl_ref, pos_ref, o_ref, slab_ref, *, lay):
    n_ctx_tiles = lay.ctx_rows // (RELAYOUT_STEPS * SEQ_GROUP)
    n_slabs = slab_ref.shape[0]

    def relayout(x_ref, pos):
        for b in range(SEQ_GROUP):
            xb = x_ref[b] if pos is None else x_ref[b] + pos
            for j in range(n_slabs):
                slab_ref[j, pl.ds(b, RELAYOUT_STEPS, stride=SEQ_GROUP), :] = (
                    xb[:, j * LANES:(j + 1) * LANES])
        for j in range(n_slabs):
            o_ref[:, j * LANES:(j + 1) * LANES] = slab_ref[j]

    @pl.when(pl.program_id(0) < n_ctx_tiles)
    def _():
        relayout(xc_ref, None)

    @pl.when(pl.program_id(0) >= n_ctx_tiles)
    def _():
        relayout(xl_ref, pos_ref[...])


def _embed(lay, x_ctx, x_lat, table):
    d = x_ctx.shape[-1]
    rows_tile = RELAYOUT_STEPS * SEQ_GROUP
    ctx, lat, lat_steps = _family_maps(lay)
    block = (SEQ_GROUP, RELAYOUT_STEPS, d)
    return pl.pallas_call(
        functools.partial(_embed_kernel, lay=lay), name="embed",
        grid=(lay.tiles(rows_tile),),
        in_specs=[pl.BlockSpec(block, ctx), pl.BlockSpec(block, lat),
                  pl.BlockSpec((RELAYOUT_STEPS, d), lat_steps)],
        out_specs=pl.BlockSpec((rows_tile, d), lambda i: (i, 0)),
        out_shape=jax.ShapeDtypeStruct((lay.rows, d), F32),
        scratch_shapes=[pltpu.VMEM((d // LANES, rows_tile, LANES), F32)],
        compiler_params=_params(),
    )(x_ctx, x_lat, table)


def _unembed_kernel(x_ref, yc_ref, yl_ref, slab_ref, *, lay):
    n_ctx_tiles = lay.ctx_rows // (RELAYOUT_STEPS * SEQ_GROUP)
    n_slabs = slab_ref.shape[0]
    for j in range(n_slabs):
        slab_ref[j] = x_ref[:, j * LANES:(j + 1) * LANES]

    def relayout(y_ref):
        for b in range(SEQ_GROUP):
            for j in range(n_slabs):
                y_ref[b, :, j * LANES:(j + 1) * LANES] = (
                    slab_ref[j, pl.ds(b, RELAYOUT_STEPS, stride=SEQ_GROUP), :])

    @pl.when(pl.program_id(0) < n_ctx_tiles)
    def _():
        relayout(yc_ref)

    @pl.when(pl.program_id(0) >= n_ctx_tiles)
    def _():
        relayout(yl_ref)


def _unembed(lay, x, shape_ctx, shape_lat):
    d = x.shape[-1]
    rows_tile = RELAYOUT_STEPS * SEQ_GROUP
    ctx, lat, _ = _family_maps(lay)
    block = (SEQ_GROUP, RELAYOUT_STEPS, d)
    return pl.pallas_call(
        functools.partial(_unembed_kernel, lay=lay), name="unembed",
        grid=(lay.tiles(rows_tile),),
        in_specs=[pl.BlockSpec((rows_tile, d), lambda i: (i, 0))],
        out_specs=[pl.BlockSpec(block, ctx), pl.BlockSpec(block, lat)],
        out_shape=[jax.ShapeDtypeStruct(shape_ctx, F32), jax.ShapeDtypeStruct(shape_lat, F32)],
        scratch_shapes=[pltpu.VMEM((d // LANES, rows_tile, LANES), F32)],
        compiler_params=_params(),
    )(x)


def _grid_sincos_table(n_tokens, d_model):
    rows = n_tokens // GRID_W
    row = jnp.repeat(jnp.arange(rows, dtype=F32), GRID_W)
    col = jnp.tile(jnp.arange(GRID_W, dtype=F32), rows)
    n_freq = d_model // 4
    omega = POS_BASE ** (-jnp.arange(n_freq, dtype=F32) / n_freq)
    ar = row[:, None] * omega
    ac = col[:, None] * omega
    return jnp.concatenate([jnp.sin(ar), jnp.cos(ar), jnp.sin(ac), jnp.cos(ac)], axis=-1)


def _mod_index(lay, tile_rows):
    def index(i):
        row = i * tile_rows
        lat = jnp.maximum(row - lay.ctx_rows, 0) // lay.len_lat
        return jnp.where(row < lay.ctx_rows, 0, 1 + lat)
    return index


def _pre_kernel(x_ref, mod_ref, gain_ref, w_ref, *out_refs, splits):
    d = x_ref.shape[-1]
    h = _rms(x_ref[...]) * gain_ref[...]
    h = _modulate(h, _mod_chunk(mod_ref, 0, d), _mod_chunk(mod_ref, 1, d)).astype(BF16)
    for (a, b, slabs), o_ref in zip(splits, out_refs):
        if slabs:
            acc = _dot(h, w_ref[:, a:b])
            for j in range((b - a) // LANES):
                o_ref[j] = acc[:, j * LANES:(j + 1) * LANES]
        else:
            o_ref[...] = _dot(h, w_ref[:, a:b])


def _pre_mixer(lay, x, mod, gain, w, splits):
    rows, d = x.shape
    n = w.shape[1]
    midx = _mod_index(lay, PRE_ROWS)
    out_specs, out_shapes = [], []
    for a, b, slabs in splits:
        if slabs:
            ns = (b - a) // LANES
            out_specs.append(pl.BlockSpec((ns, PRE_ROWS, LANES), lambda i: (0, i, 0)))
            out_shapes.append(jax.ShapeDtypeStruct((ns, rows, LANES), F32))
        else:
            out_specs.append(pl.BlockSpec((PRE_ROWS, b - a), lambda i: (i, 0)))
            out_shapes.append(jax.ShapeDtypeStruct((rows, b - a), F32))
    return pl.pallas_call(
        functools.partial(_pre_kernel, splits=splits), name="pre_mixer",
        grid=(rows // PRE_ROWS,),
        in_specs=[pl.BlockSpec((PRE_ROWS, d), lambda i: (i, 0)),
                  pl.BlockSpec((1, SEQ_GROUP, N_MOD * d), lambda i: (midx(i), 0, 0)),
                  pl.BlockSpec((1, d), lambda i: (0, 0)),
                  pl.BlockSpec((d, n), lambda i: (0, 0), pipeline_mode=pl.Buffered(1))],
        out_specs=out_specs, out_shape=out_shapes,
        compiler_params=_params(),
    )(x, mod, gain, w)


def _finish_sublayer(x, out, gain, gate8):
    return _gated_residual(x, _rms(out) * gain, gate8)


def _mlp_sublayer(x, mod_ref, gpre_ref, gpost_ref, w1_ref, w2_ref):
    d = x.shape[-1]
    dff = w1_ref.shape[-1]
    h = _rms(x) * gpre_ref[...]
    h = _modulate(h, _mod_chunk(mod_ref, 3, d), _mod_chunk(mod_ref, 4, d)).astype(BF16)
    out = None
    for a in range(0, dff, MLP_HIDDEN_BLOCK):
        f = jnp.square(jnp.maximum(_dot(h, w1_ref[0, :, a:a + MLP_HIDDEN_BLOCK]), 0.0))
        part = _dot(f.astype(BF16), w2_ref[0, a:a + MLP_HIDDEN_BLOCK, :])
        out = part if out is None else out + part
    return _finish_sublayer(x, out, gpost_ref[...], _mod_chunk(mod_ref, 5, d))


def _tail_odd_kernel(hf_ref, hb_ref, yg_ref, x_ref, mod_ref, gmix_ref, wout_ref,
                     gpre_ref, gpost_ref, w1_ref, w2_ref, o_ref):
    d = x_ref.shape[-1]
    for r0 in range(0, x_ref.shape[0], TAIL_ROW_BLOCK):
        rs = slice(r0, r0 + TAIL_ROW_BLOCK)
        y = (hf_ref[rs, :] + hb_ref[rs, :]) * jax.nn.gelu(yg_ref[rs, :])
        out = _dot(y.astype(BF16), wout_ref[...])
        x = _finish_sublayer(x_ref[rs, :], out, gmix_ref[...], _mod_chunk(mod_ref, 2, d))
        o_ref[rs, :] = _mlp_sublayer(x, mod_ref, gpre_ref, gpost_ref, w1_ref, w2_ref)


def _tail_even_kernel(u_ref, za_ref, yf_ref, yb_ref, of_ref, ob_ref, zb_ref, x_ref, mod_ref,
                      dskip_ref, onorm_ref, gmix_ref, wout_ref,
                      gpre_ref, gpost_ref, w1_ref, w2_ref, o_ref):
    d = x_ref.shape[-1]
    da = u_ref.shape[-1]
    heads, _, dv = of_ref.shape
    for r0 in range(0, x_ref.shape[0], TAIL_ROW_BLOCK):
        rs = slice(r0, r0 + TAIL_ROW_BLOCK)
        ya = jax.nn.gelu(yf_ref[rs, :] + yb_ref[rs, :] + u_ref[rs, :] * dskip_ref[...])
        ya = ya * _sigmoid(za_ref[rs, :])
        out = _dot(ya.astype(BF16), wout_ref[0:da, :])
        for h in range(heads):
            o = of_ref[h, rs, :] + ob_ref[h, rs, :]
            zb = zb_ref[rs, h * dv:(h + 1) * dv]
            oh = _rms(o) * onorm_ref[...] * (zb * _sigmoid(zb))
            out = out + _dot(oh.astype(BF16), wout_ref[da + h * dv:da + (h + 1) * dv, :])
        x = _finish_sublayer(x_ref[rs, :], out, gmix_ref[...], _mod_chunk(mod_ref, 2, d))
        o_ref[rs, :] = _mlp_sublayer(x, mod_ref, gpre_ref, gpost_ref, w1_ref, w2_ref)


def _layer_tail(lay, body, name, acts, x, mod, small, wout, gpre, gpost, w1, w2, layer):
    rows, d = x.shape
    midx = _mod_index(lay, TOKEN_ROWS)

    def act_spec(a):
        if a.ndim == 3:
            return pl.BlockSpec((a.shape[0], TOKEN_ROWS, a.shape[2]), lambda i: (0, i, 0))
        return pl.BlockSpec((TOKEN_ROWS, a.shape[1]), lambda i: (i, 0))

    const = lambda a: pl.BlockSpec(a.shape, lambda i: (0, 0))
    resident = lambda a: pl.BlockSpec(a.shape, lambda i: (0, 0), pipeline_mode=pl.Buffered(1))
    of_layer = lambda a: pl.BlockSpec((1,) + a.shape[1:], lambda i: (layer, 0, 0),
                                      pipeline_mode=pl.Buffered(1))
    return pl.pallas_call(
        body, name=name,
        grid=(rows // TOKEN_ROWS,),
        in_specs=[act_spec(a) for a in acts] + [
            act_spec(x), pl.BlockSpec((1, SEQ_GROUP, N_MOD * d), lambda i: (midx(i), 0, 0))]
            + [const(s) for s in small] + [resident(wout), const(gpre), const(gpost),
                                           of_layer(w1), of_layer(w2)],
        out_specs=act_spec(x),
        out_shape=jax.ShapeDtypeStruct((rows, d), F32),
        compiler_params=_params(),
    )(*acts, x, mod, *small, wout, gpre, gpost, w1, w2)


def _s5_discretise_kernel(lr_ref, li_ref, ldt_ref, bre_ref, bim_ref,
                          ar_ref, ai_ref, bbre_ref, bbim_ref):
    lr, li = lr_ref[...], li_ref[...]
    dt = jnp.exp(ldt_ref[...])
    mag = jnp.exp(lr * dt)
    ar = mag * jnp.cos(li * dt)
    ai = mag * jnp.sin(li * dt)
    den = lr * lr + li * li
    fr = ((ar - 1.0) * lr + ai * li) / den
    fi = (ai * lr - (ar - 1.0) * li) / den
    ar_ref[...] = ar
    ai_ref[...] = ai
    bbre_ref[...] = fr * bre_ref[...] - fi * bim_ref[...]
    bbim_ref[...] = fr * bim_ref[...] + fi * bre_ref[...]


def _s5_discretise(lam_re, lam_im, log_dt, b_re, b_im):
    e, nd, g, p = lam_re.shape
    c = b_re.shape[-1]
    full = (e, nd, g, c, p)
    flat = lambda a: jnp.broadcast_to(a, full).reshape(-1, p)
    lr = flat(lam_re[:, :, :, None, :])
    li = flat(lam_im[:, :, :, None, :])
    ldt = flat(log_dt[:, :, :, None, None])
    bre = flat(b_re.transpose(0, 1, 3, 2)[:, None])
    bim = flat(b_im.transpose(0, 1, 3, 2)[:, None])
    shape = jax.ShapeDtypeStruct(lr.shape, F32)
    ar, ai, bbre, bbim = pl.pallas_call(
        _s5_discretise_kernel, name="s5_discretise", out_shape=[shape] * 4,
        compiler_params=pltpu.CompilerParams(vmem_limit_bytes=VMEM_LIMIT_BYTES),
    )(lr, li, ldt, bre, bim)
    ar = ar.reshape(full)[:, :, :, 0, :]
    ai = ai.reshape(full)[:, :, :, 0, :]
    return ar, ai, bbre.reshape(full), bbim.reshape(full)


def _block_diag_in(bb):
    g, c, p = bb.shape
    per = S5_SLICE // c
    n = g // per
    blocks = bb.reshape(n, per, c, p)
    eye = jnp.eye(per, dtype=bb.dtype)
    return jnp.einsum('nicp,ij->nicjp', blocks, eye).reshape(n, per * c, per * p)


def _block_diag_out(cc):
    g, c, p = cc.shape
    per = S5_SLICE // c
    n = g // per
    blocks = cc.reshape(n, per, c, p)
    eye = jnp.eye(per, dtype=cc.dtype)
    return jnp.einsum('nicp,ij->nipjc', blocks, eye).reshape(n, per * p, per * c)


def _s5_scan_kernel(u_ref, h0re_ref, h0im_ref, a_ref, bw_ref, cw_ref,
                    y_ref, fre_ref, fim_ref, hre_ref, him_ref, bre_ref, bim_ref,
                    *, lay, reverse):
    _, grp, first, last, _, _ = _schedule(lay, SCAN_ROWS, reverse, pl.program_id(0))
    is_ctx = grp < lay.n_ctx
    n_slices = bw_ref.shape[0]
    wide = bw_ref.shape[2] // 2

    @pl.when(first)
    def _():
        hre_ref[...] = jnp.where(is_ctx, 0.0, h0re_ref[0])
        him_ref[...] = jnp.where(is_ctx, 0.0, h0im_ref[0])

    for j in range(n_slices):
        uj = u_ref[:, j * S5_SLICE:(j + 1) * S5_SLICE].astype(BF16)
        bb = _dot(uj, bw_ref[j])
        bre_ref[:, j * wide:(j + 1) * wide] = bb[:, :wide]
        bim_ref[:, j * wide:(j + 1) * wide] = bb[:, wide:]

    steps = range(SCAN_STEPS - 1, -1, -1) if reverse else range(SCAN_STEPS)
    for j in range(n_slices):
        cols = slice(j * wide, (j + 1) * wide)
        ar = a_ref[0, :, cols]
        ai = a_ref[1, :, cols]
        hr = hre_ref[:, cols]
        hi = him_ref[:, cols]
        for t in steps:
            rows = slice(t * SEQ_GROUP, (t + 1) * SEQ_GROUP)
            nr = ar * hr - ai * hi + bre_ref[rows, cols]
            ni = ar * hi + ai * hr + bim_ref[rows, cols]
            bre_ref[rows, cols] = nr
            bim_ref[rows, cols] = ni
            hr, hi = nr, ni
        hre_ref[:, cols] = hr
        him_ref[:, cols] = hi

    for j in range(n_slices):
        cols = slice(j * wide, (j + 1) * wide)
        yj = _dot(bre_ref[:, cols].astype(BF16), cw_ref[0, j])
        yj = yj - _dot(bim_ref[:, cols].astype(BF16), cw_ref[1, j])
        y_ref[:, j * S5_SLICE:(j + 1) * S5_SLICE] = yj

    @pl.when(last & is_ctx)
    def _():
        fre_ref[0] = hre_ref[...]
        fim_ref[0] = him_ref[...]


def _state_maps(lay, sched, ndim):
    pad = (0,) * (ndim - 1)
    start = lambda i: (jnp.maximum(sched(i)[1] - lay.n_ctx, 0),) + pad
    final = lambda i: (jnp.minimum(sched(i)[1], lay.n_ctx - 1),) + pad
    return start, final


def _s5_scan(lay, u, h0re, h0im, a8, bw, cw, reverse):
    rows, da = u.shape
    ns = h0re.shape[-1]
    sched = functools.partial(_schedule, lay, SCAN_ROWS, reverse)
    tile = lambda i: (sched(i)[0], 0)
    start, final = _state_maps(lay, sched, 3)
    state_shape = jax.ShapeDtypeStruct((lay.n_ctx, SEQ_GROUP, ns), F32)
    return pl.pallas_call(
        functools.partial(_s5_scan_kernel, lay=lay, reverse=reverse), name="s5_scan",
        grid=(lay.tiles(SCAN_ROWS),),
        in_specs=[pl.BlockSpec((SCAN_ROWS, da), tile),
                  pl.BlockSpec((1, SEQ_GROUP, ns), start),
                  pl.BlockSpec((1, SEQ_GROUP, ns), start),
                  pl.BlockSpec(a8.shape, lambda i: (0, 0, 0)),
                  pl.BlockSpec(bw.shape, lambda i: (0, 0, 0)),
                  pl.BlockSpec(cw.shape, lambda i: (0, 0, 0, 0))],
        out_specs=[pl.BlockSpec((SCAN_ROWS, da), tile),
                   pl.BlockSpec((1, SEQ_GROUP, ns), final),
                   pl.BlockSpec((1, SEQ_GROUP, ns), final)],
        out_shape=[jax.ShapeDtypeStruct((rows, da), F32), state_shape, state_shape],
        scratch_shapes=[pltpu.VMEM((SEQ_GROUP, ns), F32), pltpu.VMEM((SEQ_GROUP, ns), F32),
                        pltpu.VMEM((SCAN_ROWS, ns), F32), pltpu.VMEM((SCAN_ROWS, ns), F32)],
        compiler_params=_params(),
    )(u, h0re, h0im, a8, bw, cw)


def _halo_specs(lay, steps, channels, reverse=False, slabs=None):
    sched = functools.partial(_schedule, lay, steps * SEQ_GROUP, reverse)
    last_pair = lay.rows // (2 * SEQ_GROUP) - 1
    lead = () if slabs is None else (0,)
    shape = () if slabs is None else (slabs,)

    def prev(i):
        t = sched(i)[0]
        return lead + (jnp.maximum(t * steps - 1, 0), 0)

    def nxt(i):
        t = sched(i)[0]
        return lead + (jnp.minimum((t + 1) * (steps // 2), last_pair), 0)

    return (pl.BlockSpec(shape + (SEQ_GROUP, channels), prev),
            pl.BlockSpec(shape + (2 * SEQ_GROUP, channels), nxt))


def _conv_centred(x, prev, nxt, k, n, w, b):
    prev = jnp.where(k > 0, prev, 0.0)
    nxt = jnp.where(k < n - 1, nxt, 0.0)
    xp = jnp.concatenate([prev, x, nxt], axis=0)
    r = x.shape[0]
    out = b + xp[0:r] * w[0:1, :]
    for j in range(1, CONV_K):
        out = out + xp[j * SEQ_GROUP:j * SEQ_GROUP + r] * w[j:j + 1, :]
    return out


def _lru_scan_kernel(x_ref, xprev_ref, xnext_ref, h0_ref, cw_ref, cb_ref, wr_ref, wi_ref,
                     br_ref, bi_ref, lam_ref, h_ref, fin_ref, st_ref, a_ref, b_ref,
                     *, lay, reverse):
    _, grp, first, last, k, n = _schedule(lay, SCAN_ROWS, reverse, pl.program_id(0))
    is_ctx = grp < lay.n_ctx

    @pl.when(first)
    def _():
        st_ref[...] = jnp.where(is_ctx, 0.0, h0_ref[0])

    x = _conv_centred(x_ref[...], xprev_ref[...], xnext_ref[...], k, n, cw_ref[...], cb_ref[...])
    xb = x.astype(BF16)
    bs = wr_ref.shape[-1]
    neg_sp = -LRU_C * _softplus(-lam_ref[...])
    for blk in range(LRU_BLOCKS):
        cols = slice(blk * bs, (blk + 1) * bs)
        r = _sigmoid(_dot(xb[:, cols], wr_ref[blk]) + br_ref[:, cols])
        gi = _sigmoid(_dot(xb[:, cols], wi_ref[blk]) + bi_ref[:, cols])
        log_a = neg_sp[:, cols] * r
        th = jnp.tanh(log_a)
        a_ref[:, cols] = jnp.exp(log_a)
        y = -2.0 * th / (1.0 - th)
        root = jnp.where(y > 0.0, y * lax.rsqrt(y), 0.0)
        b_ref[:, cols] = root * (gi * x[:, cols])

    steps = range(SCAN_STEPS - 1, -1, -1) if reverse else range(SCAN_STEPS)
    h = st_ref[...]
    for t in steps:
        rows = slice(t * SEQ_GROUP, (t + 1) * SEQ_GROUP)
        h = a_ref[rows, :] * h + b_ref[rows, :]
        h_ref[rows, :] = h
    st_ref[...] = h

    @pl.when(last & is_ctx)
    def _():
        fin_ref[0] = st_ref[...]


def _lru_scan(lay, xb, h0, conv_w, conv_b, wr, wi, br, bi, lam, reverse):
    rows, d = xb.shape
    sched = functools.partial(_schedule, lay, SCAN_ROWS, reverse)
    tile = lambda i: (sched(i)[0], 0)
    start, final = _state_maps(lay, sched, 3)
    const2 = lambda a: pl.BlockSpec(a.shape, lambda i: (0, 0))
    const3 = lambda a: pl.BlockSpec(a.shape, lambda i: (0, 0, 0))
    prev_spec, next_spec = _halo_specs(lay, SCAN_STEPS, d, reverse)
    return pl.pallas_call(
        functools.partial(_lru_scan_kernel, lay=lay, reverse=reverse), name="lru_scan",
        grid=(lay.tiles(SCAN_ROWS),),
        in_specs=[pl.BlockSpec((SCAN_ROWS, d), tile), prev_spec, next_spec,
                  pl.BlockSpec((1, SEQ_GROUP, d), start),
                  const2(conv_w), const2(conv_b), const3(wr), const3(wi),
                  const2(br), const2(bi), const2(lam)],
        out_specs=[pl.BlockSpec((SCAN_ROWS, d), tile), pl.BlockSpec((1, SEQ_GROUP, d), final)],
        out_shape=[jax.ShapeDtypeStruct((rows, d), F32),
                   jax.ShapeDtypeStruct((lay.n_ctx, SEQ_GROUP, d), F32)],
        scratch_shapes=[pltpu.VMEM((SEQ_GROUP, d), F32),
                        pltpu.VMEM((SCAN_ROWS, d), F32), pltpu.VMEM((SCAN_ROWS, d), F32)],
        compiler_params=_params(),
    )(xb, xb, xb, h0, conv_w, conv_b, wr, wi, br, bi, lam)


def _unit_triangular_inverses(lmats, eye, blocks):
    c = lmats[0].shape[0]
    ts = [jnp.where(blocks[0], -l, 0.0) for l in lmats]
    tbs = [t.astype(BF16) for t in ts]
    invs = [eye + t for t in ts]
    ps = [_dot(tb, tb) for tb in tbs]
    ys = [_dot(jnp.concatenate([x, p], axis=0).astype(BF16), p.astype(BF16))
          for x, p in zip(invs, ps)]
    invs = [x + y[:c] for x, y in zip(invs, ys)]
    zs = [_dot(x.astype(BF16), y[c:].astype(BF16)) for x, y in zip(invs, ys)]
    invs = [x + z for x, z in zip(invs, zs)]
    inside = blocks[0]
    for outer in blocks[1:] + [None]:
        off = ~inside if outer is None else (outer & ~inside)
        es = [jnp.where(off, l, 0.0).astype(BF16) for l in lmats]
        xbs = [x.astype(BF16) for x in invs]
        xes = [_dot(xb, e).astype(BF16) for xb, e in zip(xbs, es)]
        xexs = [_dot(xe, xb) for xe, xb in zip(xes, xbs)]
        invs = [x - xex for x, xex in zip(invs, xexs)]
        inside = outer
    return invs


def _gdn_prep_kernel(x_ref, xprev_ref, xnext_ref, ab_ref, cw_ref, cb_ref, alog_ref, dtb_ref,
                     s0_ref, o_ref, fin_ref, qkv_ref, gate_ref, s_ref, *, lay, reverse, direction):
    _, _, _, _, kpos, npos = _schedule(lay, GDN_TILE_ROWS, reverse, pl.program_id(0))
    dk = x_ref.shape[-1]
    for j in range(3 * GDN_HEADS):
        x = _conv_centred(x_ref[j], xprev_ref[j], xnext_ref[j], kpos, npos, cw_ref[j], cb_ref[j])
        x = x * _sigmoid(x)
        if j < 2 * GDN_HEADS:
            x = x * lax.rsqrt(jnp.sum(x * x, axis=-1, keepdims=True) + EPS)
        if j < GDN_HEADS:
            x = x * (dk ** -0.5)
        qkv_ref[j] = x
    ab = ab_ref[...]
    g_all = -jnp.exp(alog_ref[...]) * _softplus(ab + dtb_ref[...])
    lane_id = lax.broadcasted_iota(jnp.int32, ab.shape, 1)
    gate_ref[...] = jnp.where(lane_id < 2 * GDN_HEADS, g_all, _sigmoid(ab))
    _gdn_kernel(qkv_ref, gate_ref, s0_ref, o_ref, fin_ref, s_ref,
                lay=lay, reverse=reverse, direction=direction)


def _gdn_kernel(qkv_ref, gate_ref, s0_ref, o_ref, fin_ref, s_ref, *, lay, reverse, direction):
    _, grp, first, last, _, _ = _schedule(lay, GDN_TILE_ROWS, reverse, pl.program_id(0))
    is_ctx = grp < lay.n_ctx
    c = GDN_CHUNK

    @pl.when(first)
    def _():
        s_ref[...] = jnp.where(is_ctx, 0.0, s0_ref[0])

    ri = lax.broadcasted_iota(jnp.int32, (c, c), 0)
    ci = lax.broadcasted_iota(jnp.int32, (c, c), 1)
    incl = (ri <= ci) if reverse else (ri >= ci)
    strict = (ri < ci) if reverse else (ri > ci)
    tri = jnp.where(incl, 1.0, 0.0).astype(F32)
    eye = jnp.where(ri == ci, 1.0, 0.0).astype(F32)
    blocks = []
    shift = 3
    while (1 << shift) < c:
        blocks.append((ri >> shift) == (ci >> shift))
        shift += 1
    end_row = 0 if reverse else c - 1
    seq_rows = lambda b: pl.ds(b, c, stride=SEQ_GROUP)
    for b0 in range(0, SEQ_GROUP, GDN_SEQ_BATCH):
        _delta_rule_chunk(range(b0, b0 + GDN_SEQ_BATCH), qkv_ref, gate_ref, s_ref, o_ref, seq_rows,
                          (incl, strict, tri, eye, blocks), end_row, direction)

    @pl.when(last & is_ctx)
    def _():
        fin_ref[0] = s_ref[...]


def _delta_rule_chunk(seqs, qkv_ref, gate_ref, s_ref, o_ref, seq_rows, masks, end_row, direction):
    incl, strict, tri, eye, blocks = masks
    c = GDN_CHUNK
    dk = qkv_ref.shape[-1]
    keys = [(b, h) for b in seqs for h in range(GDN_HEADS)]

    gcols, grams, decays, kbs, qes, betas = {}, {}, {}, {}, {}, {}
    for b in seqs:
        gates = gate_ref[seq_rows(b), :]
        gc = _dot_hi(tri, gates)
        gc_t = gc.T
        for h in range(GDN_HEADS):
            lane = direction * GDN_HEADS + h
            q = qkv_ref[h, seq_rows(b), :]
            k = qkv_ref[GDN_HEADS + h, seq_rows(b), :]
            beta = gates[:, 2 * GDN_HEADS + lane:2 * GDN_HEADS + lane + 1]
            gcol = gc[:, lane:lane + 1]
            grow = gc_t[lane:lane + 1, :]
            egc = jnp.exp(gcol)
            gcols[b, h] = gcol
            betas[b, h] = beta
            decays[b, h] = jnp.exp(jnp.where(incl, gcol - grow, -1e30))
            kb = k * beta
            kbs[b, h] = kb * egc
            qes[b, h] = q * egc
            grams[b, h] = _dot_nt(jnp.concatenate([kb, q], axis=0).astype(BF16), k.astype(BF16))
    lmats = [jnp.where(strict, grams[key][:c] * decays[key], 0.0) for key in keys]
    invs = _unit_triangular_inverses(lmats, eye, blocks)
    uws = {}
    for key, inv in zip(keys, invs):
        b, h = key
        vb = qkv_ref[2 * GDN_HEADS + h, seq_rows(b), :] * betas[key]
        rhs = jnp.concatenate([vb, kbs[key]], axis=1)
        uws[key] = _dot(inv.astype(BF16), rhs.astype(BF16))

    ss = {key: s_ref[key[0], key[1]] for key in keys}
    projs = {key: _dot(jnp.concatenate([uws[key][:, dk:], qes[key]], axis=0).astype(BF16),
                       ss[key].astype(BF16)) for key in keys}
    v_news = {key: uws[key][:, :dk] - projs[key][:c] for key in keys}
    for key in keys:
        b, h = key
        k = qkv_ref[GDN_HEADS + h, seq_rows(b), :]
        g_end = gcols[key][end_row:end_row + 1, :]
        kd = k * jnp.exp(g_end - gcols[key])
        s_ref[b, h] = ss[key] * jnp.exp(g_end) + _dot_tn(kd.astype(BF16),
                                                         v_news[key].astype(BF16))
    for key in keys:
        b, h = key
        qk = jnp.where(incl, grams[key][c:] * decays[key], 0.0)
        o_ref[h, seq_rows(b), :] = projs[key][c:] + _dot(qk.astype(BF16),
                                                         v_news[key].astype(BF16))


def _gdn_scan(lay, qkv, gates, s0, reverse, direction, prep=None):
    slabs, rows, dk = qkv.shape
    sched = functools.partial(_schedule, lay, GDN_TILE_ROWS, reverse)
    tile3 = lambda i: (0, sched(i)[0], 0)
    tile2 = lambda i: (sched(i)[0], 0)
    start, final = _state_maps(lay, sched, 5)
    sblock = (1,) + s0.shape[1:]
    qkv_spec = pl.BlockSpec((slabs, GDN_TILE_ROWS, dk), tile3)
    gate_spec = pl.BlockSpec((GDN_TILE_ROWS, LANES), tile2)
    out_specs = [pl.BlockSpec((GDN_HEADS, GDN_TILE_ROWS, dk), tile3), pl.BlockSpec(sblock, final)]
    out_shape = [jax.ShapeDtypeStruct((GDN_HEADS, rows, dk), F32),
                 jax.ShapeDtypeStruct((lay.n_ctx,) + s0.shape[1:], F32)]
    static = dict(lay=lay, reverse=reverse, direction=direction)
    if prep is None:
        body, name = functools.partial(_gdn_kernel, **static), "gdn_scan"
        in_specs = [qkv_spec, gate_spec, pl.BlockSpec(sblock, start)]
        args = (qkv, gates, s0)
    else:
        body, name = functools.partial(_gdn_prep_kernel, **static), "gdn_prep_scan"
        const2 = lambda a: pl.BlockSpec(a.shape, lambda i: (0, 0))
        const3 = lambda a: pl.BlockSpec(a.shape, lambda i: (0, 0, 0))
        prev_spec, next_spec = _halo_specs(lay, GDN_CHUNK, dk, reverse, slabs=slabs)
        conv_w, conv_b, alog, dtb = prep
        in_specs = [qkv_spec, prev_spec, next_spec, gate_spec, const3(conv_w), const3(conv_b),
                    const2(alog), const2(dtb), pl.BlockSpec(sblock, start)]
        args = (qkv, qkv, qkv, gates, conv_w, conv_b, alog, dtb, s0)
        out_specs += [qkv_spec, gate_spec]
        out_shape += [jax.ShapeDtypeStruct(qkv.shape, F32), jax.ShapeDtypeStruct(gates.shape, F32)]
    return pl.pallas_call(
        body, name=name, grid=(lay.tiles(GDN_TILE_ROWS),),
        in_specs=in_specs, out_specs=out_specs, out_shape=out_shape,
        scratch_shapes=[pltpu.VMEM(s0.shape[1:], F32)],
        compiler_params=_params(),
    )(*args)


def _group_states(lat_state, flatten=True):
    db = lat_state.shape[0]
    tail = (-1,) if flatten else lat_state.shape[1:]
    return lat_state.reshape((db // SEQ_GROUP, SEQ_GROUP) + tail).astype(F32)


def kernel(x_prompt, x_sample, state_s5_re, state_s5_im, state_delta, state_lru, c, c_ctx, w_ada, b_ada, norm_mix_pre, norm_mix_post, norm_mlp_pre, norm_mlp_post, w_mlp_in, w_mlp_out, w_in_even, w_out_even, s5_lam_re, s5_lam_im, s5_log_dt, s5_b_re, s5_b_im, s5_c_re, s5_c_im, s5_d, gdn_conv_w, gdn_conv_b, gdn_a_log, gdn_dt_bias, gdn_o_norm, w_in_odd, w_out_odd, lru_conv_w, lru_conv_b, lru_w_r, lru_b_r, lru_w_i, lru_b_i, lru_lam):
    bp, tp, d = x_prompt.shape
    bl, tl, _ = x_sample.shape
    depth = w_ada.shape[0]
    n_dir = 2
    da = s5_d.shape[1]
    db = gdn_conv_w.shape[2] // 3
    heads = GDN_HEADS
    ngrp = s5_lam_re.shape[2]

    lay = Layout(bp // SEQ_GROUP, tp * SEQ_GROUP, bl // SEQ_GROUP, tl * SEQ_GROUP)

    cond = jnp.concatenate([c_ctx[None].astype(F32), c.astype(F32)], axis=0)
    pad = (-cond.shape[0]) % SUBLANES
    cond = jnp.pad(cond, ((0, pad), (0, 0)))
    ada = _ada_vectors(cond, w_ada, b_ada)
    mod_ctx = jnp.broadcast_to(ada[:, 0:1], (depth, SEQ_GROUP, ada.shape[-1]))[:, None]
    mod_lat = ada[:, 1:1 + bl].reshape(depth, bl // SEQ_GROUP, SEQ_GROUP, -1)
    mod = jnp.concatenate([mod_ctx, mod_lat], axis=1)

    x = _embed(lay, x_prompt.astype(F32), x_sample.astype(F32), _grid_sincos_table(tl, d))

    a_re, a_im, bb_re, bb_im = _s5_discretise(s5_lam_re, s5_lam_im, s5_log_dt, s5_b_re, s5_b_im)
    c_re_t = s5_c_re
    c_im_t = s5_c_im

    row = lambda v: v.reshape(1, -1).astype(F32)
    w_mlp_in_bf = w_mlp_in.astype(BF16)
    w_mlp_out_bf = w_mlp_out.astype(BF16)
    new_re, new_im, new_delta, new_lru = [], [], [], []
    for l in range(depth):
        mod_l = mod[l]
        if l % 2 == 0:
            e = l // 2
            n_in = w_in_even.shape[2]
            n_pad = (-n_in) % LANES
            w_in = jnp.pad(w_in_even[e], ((0, 0), (0, n_pad))).astype(BF16)
            splits = ((0, da, False), (da, 2 * da, False), (2 * da, 2 * da + 3 * db, True),
                      (2 * da + 3 * db, 2 * da + 4 * db, False),
                      (2 * da + 4 * db, n_in + n_pad, False))
            u, za, qkv, zb, ab = _pre_mixer(lay, x, mod_l, row(norm_mix_pre[l]), w_in, splits)

            cw = jnp.stack([_block_diag_out(c_re_t[e]), _block_diag_out(c_im_t[e])]).astype(BF16)
            ys, fr, fi = [], [], []
            for dd in range(n_dir):
                a8 = jnp.stack([a_re[e, dd].reshape(-1), a_im[e, dd].reshape(-1)])
                a8 = jnp.broadcast_to(a8[:, None, :], (2, SEQ_GROUP, a8.shape[-1]))
                bw = jnp.concatenate([_block_diag_in(bb_re[e, dd]), _block_diag_in(bb_im[e, dd])],
                                     axis=-1).astype(BF16)
                y, f_re, f_im = _s5_scan(lay, u, _group_states(state_s5_re[:, e, dd]),
                                         _group_states(state_s5_im[:, e, dd]),
                                         a8, bw, cw, reverse=(dd == 1))
                ys.append(y)
                fr.append(f_re.reshape(bp, ngrp, -1))
                fi.append(f_im.reshape(bp, ngrp, -1))
            new_re.append(jnp.stack(fr, axis=1))
            new_im.append(jnp.stack(fi, axis=1))

            pad8 = lambda v: jnp.pad(v.reshape(1, -1).astype(F32), ((0, 0), (0, LANES - v.size)))
            n_slabs = 3 * db // LANES
            conv_w = gdn_conv_w[e].astype(F32).reshape(CONV_K, n_slabs, LANES).transpose(1, 0, 2)
            conv_b = gdn_conv_b[e].astype(F32).reshape(n_slabs, 1, LANES)
            os_, fd = [], []
            prep = (conv_w, conv_b, pad8(gdn_a_log[e]), pad8(gdn_dt_bias[e]))
            for dd in range(n_dir):
                s0 = _group_states(state_delta[:, e, dd], flatten=False)
                if dd == 0:
                    o_d, s_fin, qkv, ab = _gdn_scan(lay, qkv, ab, s0, False, dd, prep=prep)
                else:
                    o_d, s_fin = _gdn_scan(lay, qkv, ab, s0, True, dd)
                os_.append(o_d)
                fd.append(s_fin.reshape((bp,) + s_fin.shape[2:]))
            new_delta.append(jnp.stack(fd, axis=1))

            acts = [u, za, ys[0], ys[1], os_[0], os_[1], zb]
            small = [row(s5_d[e]), row(gdn_o_norm[e]), row(norm_mix_post[l])]
            body, name, w_out = _tail_even_kernel, "tail_even", w_out_even[e]
        else:
            o = l // 2
            dr = lru_lam.shape[2]
            w_in = w_in_odd[o].astype(BF16)
            xb, yg = _pre_mixer(lay, x, mod_l, row(norm_mix_pre[l]), w_in,
                                ((0, dr, False), (dr, 2 * dr, False)))
            hs, fl = [], []
            for dd in range(n_dir):
                h, h_fin = _lru_scan(lay, xb, _group_states(state_lru[:, o, dd]),
                                     lru_conv_w[o].astype(F32), row(lru_conv_b[o]),
                                     lru_w_r[o, dd].astype(BF16), lru_w_i[o, dd].astype(BF16),
                                     row(lru_b_r[o, dd]), row(lru_b_i[o, dd]), row(lru_lam[o, dd]),
                                     reverse=(dd == 1))
                hs.append(h)
                fl.append(h_fin.reshape(bp, dr))
            new_lru.append(jnp.stack(fl, axis=1))
            acts = [hs[0], hs[1], yg]
            small = [row(norm_mix_post[l])]
            body, name, w_out = _tail_odd_kernel, "tail_odd", w_out_odd[o]

        x = _layer_tail(lay, body, name, acts, x, mod_l, small, w_out.astype(BF16),
                        row(norm_mlp_pre[l]), row(norm_mlp_post[l]), w_mlp_in_bf, w_mlp_out_bf, l)

    y_prompt, y_sample = _unembed(lay, x, x_prompt.shape, x_sample.shape)
    y_prompt = y_prompt.astype(x_prompt.dtype)
    y_sample = y_sample.astype(x_sample.dtype)
    p_a = s5_lam_re.shape[3]
    new_s5_re = jnp.stack(new_re, axis=1).reshape(bp, -1, n_dir, ngrp, p_a)
    new_s5_im = jnp.stack(new_im, axis=1).reshape(bp, -1, n_dir, ngrp, p_a)
    return (y_prompt, y_sample, new_s5_re, new_s5_im,
            jnp.stack(new_delta, axis=1), jnp.stack(new_lru, axis=1))
```

```python
import functools
from typing import NamedTuple

import jax
import jax.numpy as jnp
from jax import lax
from jax.experimental import pallas as pl
from jax.experimental.pallas import tpu as pltpu

F32 = jnp.float32
BF16 = jnp.bfloat16
HIGHEST = lax.Precision.HIGHEST

EPS = 1e-6
POS_BASE = 10000.0
GRID_W = 64
N_MOD = 6
S5_GROUP = 16
S5_STATES = 64
GDN_HEADS = 4
GDN_CHUNK = 64
CONV_K = 4
CONV_LEFT = (CONV_K - 1) // 2
LRU_BLOCKS = 4
LRU_C = 8.0

SUBLANES = 8
LANES = 128
SEQ_GROUP = SUBLANES
SCAN_STEPS = 32
SCAN_ROWS = SCAN_STEPS * SEQ_GROUP
GDN_TILE_ROWS = GDN_CHUNK * SEQ_GROUP
GDN_SEQ_BATCH = 8
RELAYOUT_STEPS = 64
TOKEN_ROWS = 512
PRE_ROWS = 1024
MLP_HIDDEN_BLOCK = 2048
TAIL_ROW_BLOCK = 512
S5_SLICE = 128
VMEM_LIMIT_BYTES = 56 * 1024 * 1024


class Layout(NamedTuple):
    n_ctx: int
    len_ctx: int
    n_lat: int
    len_lat: int

    @property
    def ctx_rows(self):
        return self.n_ctx * self.len_ctx

    @property
    def rows(self):
        return self.ctx_rows + self.n_lat * self.len_lat

    @property
    def groups(self):
        return self.n_ctx + self.n_lat

    def tiles(self, tile_rows):
        return self.ctx_rows // tile_rows + self.n_lat * (self.len_lat // tile_rows)


def _schedule(lay, tile_rows, reverse, i):
    nc = lay.len_ctx // tile_rows
    nl = lay.len_lat // tile_rows
    n_ctx_tiles = lay.n_ctx * nc
    is_ctx = i < n_ctx_tiles
    j = jnp.maximum(i - n_ctx_tiles, 0)
    ic = jnp.minimum(i, n_ctx_tiles - 1)
    g = jnp.where(is_ctx, ic // nc, lay.n_ctx + j // nl)
    step = jnp.where(is_ctx, ic % nc, j % nl)
    n = jnp.where(is_ctx, nc, nl)
    k = (n - 1 - step) if reverse else step
    base = jnp.where(is_ctx, (ic // nc) * nc, n_ctx_tiles + (j // nl) * nl)
    return base + k, g, step == 0, step == n - 1, k, n


def _params(n_axes=1):
    return pltpu.CompilerParams(dimension_semantics=("arbitrary",) * n_axes,
                                vmem_limit_bytes=VMEM_LIMIT_BYTES)


def _dot(a, b):
    return jnp.dot(a, b, preferred_element_type=F32)


def _dot_hi(a, b):
    return jnp.dot(a, b, precision=HIGHEST, preferred_element_type=F32)


def _dot_nt(a, b):
    return lax.dot_general(a, b, (((1,), (1,)), ((), ())), preferred_element_type=F32)


def _dot_tn(a, b):
    return lax.dot_general(a, b, (((0,), (0,)), ((), ())), preferred_element_type=F32)


def _rms(x):
    return x * lax.rsqrt(jnp.mean(x * x, axis=-1, keepdims=True) + EPS)


def _per_group(rows_val, vec8):
    r, c = rows_val.shape
    return rows_val.reshape(r // SEQ_GROUP, SEQ_GROUP, c), vec8[None]


def _modulate(h, shift8, scale8):
    h3, sc = _per_group(h, scale8)
    out = h3 * (1.0 + sc) + shift8[None]
    return out.reshape(h.shape)


def _gated_residual(x, y, gate8):
    y3, g = _per_group(y, gate8)
    return x + (y3 * g).reshape(x.shape)


def _mod_chunk(mod_ref, idx, d):
    return mod_ref[0, :, idx * d:(idx + 1) * d]


def _softplus(x):
    return jnp.maximum(x, 0.0) + jnp.log1p(jnp.exp(-jnp.abs(x)))


def _sigmoid(x):
    return 0.5 * jnp.tanh(0.5 * x) + 0.5


def _ada_kernel(c_ref, w_ref, b_ref, o_ref):
    s = jax.nn.silu(c_ref[...]).astype(BF16)
    o_ref[0] = _dot(s, w_ref[0].astype(BF16)) + b_ref[0]


def _ada_vectors(cond, w_ada, b_ada):
    depth, d, n = w_ada.shape
    rows = cond.shape[0]
    tn = 512
    return pl.pallas_call(
        _ada_kernel, name="ada_vectors",
        grid=(depth, n // tn),
        in_specs=[pl.BlockSpec((rows, d), lambda l, j: (0, 0)),
                  pl.BlockSpec((1, d, tn), lambda l, j: (l, 0, j)),
                  pl.BlockSpec((1, 1, tn), lambda l, j: (l, 0, j))],
        out_specs=pl.BlockSpec((1, rows, tn), lambda l, j: (l, 0, j)),
        out_shape=jax.ShapeDtypeStruct((depth, rows, n), F32),
        compiler_params=_params(2),
    )(cond, w_ada, b_ada.reshape(depth, 1, n))


def _family_maps(lay):
    sched = functools.partial(_schedule, lay, RELAYOUT_STEPS * SEQ_GROUP, False)
    nc = lay.len_ctx // (RELAYOUT_STEPS * SEQ_GROUP)

    def ctx(i):
        _, g, _, _, k, _ = sched(i)
        on = g < lay.n_ctx
        return (jnp.minimum(g, lay.n_ctx - 1), jnp.where(on, k, nc - 1), 0)

    def lat(i):
        _, g, _, _, k, _ = sched(i)
        on = g >= lay.n_ctx
        return (jnp.maximum(g - lay.n_ctx, 0), jnp.where(on, k, 0), 0)

    def lat_steps(i):
        return lat(i)[1:]

    return ctx, lat, lat_steps


def _embed_kernel(xc_ref, xl_ref, pos_ref, o_ref, slab_ref, *, lay):
    n_ctx_tiles = lay.ctx_rows // (RELAYOUT_STEPS * SEQ_GROUP)
    n_slabs = slab_ref.shape[0]

    def relayout(x_ref, pos):
        for b in range(SEQ_GROUP):
            xb = x_ref[b] if pos is None else x_ref[b] + pos
            for j in range(n_slabs):
                slab_ref[j, pl.ds(b, RELAYOUT_STEPS, stride=SEQ_GROUP), :] = (
                    xb[:, j * LANES:(j + 1) * LANES])
        for j in range(n_slabs):
            o_ref[:, j * LANES:(j + 1) * LANES] = slab_ref[j]

    @pl.when(pl.program_id(0) < n_ctx_tiles)
    def _():
        relayout(xc_ref, None)

    @pl.when(pl.program_id(0) >= n_ctx_tiles)
    def _():
        relayout(xl_ref, pos_ref[...])


def _embed(lay, x_ctx, x_lat, table):
    d = x_ctx.shape[-1]
    rows_tile = RELAYOUT_STEPS * SEQ_GROUP
    ctx, lat, lat_steps = _family_maps(lay)
    block = (SEQ_GROUP, RELAYOUT_STEPS, d)
    return pl.pallas_call(
        functools.partial(_embed_kernel, lay=lay), name="embed",
        grid=(lay.tiles(rows_tile),),
        in_specs=[pl.BlockSpec(block, ctx), pl.BlockSpec(block, lat),
                  pl.BlockSpec((RELAYOUT_STEPS, d), lat_steps)],
        out_specs=pl.BlockSpec((rows_tile, d), lambda i: (i, 0)),
        out_shape=jax.ShapeDtypeStruct((lay.rows, d), F32),
        scratch_shapes=[pltpu.VMEM((d // LANES, rows_tile, LANES), F32)],
        compiler_params=_params(),
    )(x_ctx, x_lat, table)


def _unembed_kernel(x_ref, yc_ref, yl_ref, slab_ref, *, lay):
    n_ctx_tiles = lay.ctx_rows // (RELAYOUT_STEPS * SEQ_GROUP)
    n_slabs = slab_ref.shape[0]
    for j in range(n_slabs):
        slab_ref[j] = x_ref[:, j * LANES:(j + 1) * LANES]

    def relayout(y_ref):
        for b in range(SEQ_GROUP):
            for j in range(n_slabs):
                y_ref[b, :, j * LANES:(j + 1) * LANES] = (
                    slab_ref[j, pl.ds(b, RELAYOUT_STEPS, stride=SEQ_GROUP), :])

    @pl.when(pl.program_id(0) < n_ctx_tiles)
    def _():
        relayout(yc_ref)

    @pl.when(pl.program_id(0) >= n_ctx_tiles)
    def _():
        relayout(yl_ref)


def _unembed(lay, x, shape_ctx, shape_lat):
    d = x.shape[-1]
    rows_tile = RELAYOUT_STEPS * SEQ_GROUP
    ctx, lat, _ = _family_maps(lay)
    block = (SEQ_GROUP, RELAYOUT_STEPS, d)
    return pl.pallas_call(
        functools.partial(_unembed_kernel, lay=lay), name="unembed",
        grid=(lay.tiles(rows_tile),),
        in_specs=[pl.BlockSpec((rows_tile, d), lambda i: (i, 0))],
        out_specs=[pl.BlockSpec(block, ctx), pl.BlockSpec(block, lat)],
        out_shape=[jax.ShapeDtypeStruct(shape_ctx, F32), jax.ShapeDtypeStruct(shape_lat, F32)],
        scratch_shapes=[pltpu.VMEM((d // LANES, rows_tile, LANES), F32)],
        compiler_params=_params(),
    )(x)


def _grid_sincos_table(n_tokens, d_model):
    rows = n_tokens // GRID_W
    row = jnp.repeat(jnp.arange(rows, dtype=F32), GRID_W)
    col = jnp.tile(jnp.arange(GRID_W, dtype=F32), rows)
    n_freq = d_model // 4
    omega = POS_BASE ** (-jnp.arange(n_freq, dtype=F32) / n_freq)
    ar = row[:, None] * omega
    ac = col[:, None] * omega
    return jnp.concatenate([jnp.sin(ar), jnp.cos(ar), jnp.sin(ac), jnp.cos(ac)], axis=-1)


def _mod_index(lay, tile_rows):
    def index(i):
        row = i * tile_rows
        lat = jnp.maximum(row - lay.ctx_rows, 0) // lay.len_lat
        return jnp.where(row < lay.ctx_rows, 0, 1 + lat)
    return index


def _pre_kernel(x_ref, xprev_ref, xnext_ref, mod_ref, gain_ref, w_ref, *refs, splits, lay):
    n_par = 2 * sum(kind != "plain" for _, _, kind in splits)
    par_refs, out_refs = refs[:n_par], refs[n_par:]
    _, _, _, _, kpos, npos = _schedule(lay, PRE_ROWS, False, pl.program_id(0))
    r, d = x_ref.shape
    lead = xnext_ref.shape[0]
    xall = jnp.concatenate([xnext_ref[...], x_ref[...], xprev_ref[...],
                            jnp.zeros_like(xprev_ref[...])], axis=0)
    h = _rms(xall) * gain_ref[...]
    h = _modulate(h, _mod_chunk(mod_ref, 0, d), _mod_chunk(mod_ref, 1, d)).astype(BF16)
    hmain = h[lead:lead + r]
    p = 0
    for (a, b, kind), o_ref in zip(splits, out_refs):
        if kind == "plain":
            o_ref[...] = _dot(hmain, w_ref[:, a:b])
            continue
        p0_ref, p1_ref = par_refs[p:p + 2]
        p += 2
        if kind == "gates":
            ab = _dot(hmain, w_ref[:, a:b])
            g_all = -jnp.exp(p0_ref[...]) * _softplus(ab + p1_ref[...])
            lane_id = lax.broadcasted_iota(jnp.int32, ab.shape, 1)
            o_ref[...] = jnp.where(lane_id < 2 * GDN_HEADS, g_all, _sigmoid(ab))
            continue
        acc = _dot(h, w_ref[:, a:b])
        nxt, main, prev = acc[0:lead], acc[lead:lead + r], acc[lead + r:lead + r + SEQ_GROUP]
        if kind == "conv":
            o_ref[...] = _conv_centred(main, prev, nxt, kpos, npos, p0_ref[...], p1_ref[...])
            continue
        dk = LANES
        for j in range((b - a) // dk):
            cols = slice(j * dk, (j + 1) * dk)
            x = _conv_centred(main[:, cols], prev[:, cols], nxt[:, cols], kpos, npos,
                              p0_ref[j], p1_ref[j])
            x = x * _sigmoid(x)
            if j < 2 * GDN_HEADS:
                x = x * lax.rsqrt(jnp.sum(x * x, axis=-1, keepdims=True) + EPS)
            if j < GDN_HEADS:
                x = x * (dk ** -0.5)
            o_ref[j] = x


def _pre_mixer(lay, x, mod, gain, w, splits, params):
    rows, d = x.shape
    n = w.shape[1]
    midx = _mod_index(lay, PRE_ROWS)
    prev_spec, next_spec = _halo_specs(lay, PRE_ROWS // SEQ_GROUP, d)
    out_specs, out_shapes = [], []
    for a, b, kind in splits:
        if kind == "qkv":
            ns = (b - a) // LANES
            out_specs.append(pl.BlockSpec((ns, PRE_ROWS, LANES), lambda i: (0, i, 0)))
            out_shapes.append(jax.ShapeDtypeStruct((ns, rows, LANES), F32))
        else:
            out_specs.append(pl.BlockSpec((PRE_ROWS, b - a), lambda i: (i, 0)))
            out_shapes.append(jax.ShapeDtypeStruct((rows, b - a), F32))
    const = lambda a: pl.BlockSpec(a.shape, lambda i: (0,) * a.ndim)
    return pl.pallas_call(
        functools.partial(_pre_kernel, splits=splits, lay=lay), name="pre_mixer",
        grid=(rows // PRE_ROWS,),
        in_specs=[pl.BlockSpec((PRE_ROWS, d), lambda i: (i, 0)), prev_spec, next_spec,
                  pl.BlockSpec((1, SEQ_GROUP, N_MOD * d), lambda i: (midx(i), 0, 0)),
                  pl.BlockSpec((1, d), lambda i: (0, 0)),
                  pl.BlockSpec((d, n), lambda i: (0, 0), pipeline_mode=pl.Buffered(1))]
                 + [const(a) for a in params],
        out_specs=out_specs, out_shape=out_shapes,
        compiler_params=_params(),
    )(x, x, x, mod, gain, w, *params)


def _finish_sublayer(x, out, gain, gate8):
    return _gated_residual(x, _rms(out) * gain, gate8)


def _mlp_sublayer(x, mod_ref, gpre_ref, gpost_ref, w1_ref, w2_ref):
    d = x.shape[-1]
    dff = w1_ref.shape[-1]
    h = _rms(x) * gpre_ref[...]
    h = _modulate(h, _mod_chunk(mod_ref, 3, d), _mod_chunk(mod_ref, 4, d)).astype(BF16)
    out = None
    for a in range(0, dff, MLP_HIDDEN_BLOCK):
        f = jnp.square(jnp.maximum(_dot(h, w1_ref[0, :, a:a + MLP_HIDDEN_BLOCK]), 0.0))
        part = _dot(f.astype(BF16), w2_ref[0, a:a + MLP_HIDDEN_BLOCK, :])
        out = part if out is None else out + part
    return _finish_sublayer(x, out, gpost_ref[...], _mod_chunk(mod_ref, 5, d))


def _tail_odd_kernel(hf_ref, hb_ref, yg_ref, x_ref, mod_ref, gmix_ref, wout_ref,
                     gpre_ref, gpost_ref, w1_ref, w2_ref, o_ref):
    d = x_ref.shape[-1]
    for r0 in range(0, x_ref.shape[0], TAIL_ROW_BLOCK):
        rs = slice(r0, r0 + TAIL_ROW_BLOCK)
        y = (hf_ref[rs, :] + hb_ref[rs, :]) * jax.nn.gelu(yg_ref[rs, :])
        out = _dot(y.astype(BF16), wout_ref[...])
        x = _finish_sublayer(x_ref[rs, :], out, gmix_ref[...], _mod_chunk(mod_ref, 2, d))
        o_ref[rs, :] = _mlp_sublayer(x, mod_ref, gpre_ref, gpost_ref, w1_ref, w2_ref)


def _tail_even_kernel(u_ref, za_ref, yf_ref, yb_ref, of_ref, ob_ref, zb_ref, x_ref, mod_ref,
                      dskip_ref, onorm_ref, gmix_ref, wout_ref,
                      gpre_ref, gpost_ref, w1_ref, w2_ref, o_ref):
    d = x_ref.shape[-1]
    da = u_ref.shape[-1]
    heads, _, dv = of_ref.shape
    for r0 in range(0, x_ref.shape[0], TAIL_ROW_BLOCK):
        rs = slice(r0, r0 + TAIL_ROW_BLOCK)
        ya = jax.nn.gelu(yf_ref[rs, :] + yb_ref[rs, :] + u_ref[rs, :] * dskip_ref[...])
        ya = ya * _sigmoid(za_ref[rs, :])
        out = _dot(ya.astype(BF16), wout_ref[0:da, :])
        for h in range(heads):
            o = of_ref[h, rs, :] + ob_ref[h, rs, :]
            zb = zb_ref[rs, h * dv:(h + 1) * dv]
            oh = _rms(o) * onorm_ref[...] * (zb * _sigmoid(zb))
            out = out + _dot(oh.astype(BF16), wout_ref[da + h * dv:da + (h + 1) * dv, :])
        x = _finish_sublayer(x_ref[rs, :], out, gmix_ref[...], _mod_chunk(mod_ref, 2, d))
        o_ref[rs, :] = _mlp_sublayer(x, mod_ref, gpre_ref, gpost_ref, w1_ref, w2_ref)


def _layer_tail(lay, body, name, acts, x, mod, small, wout, gpre, gpost, w1, w2, layer):
    rows, d = x.shape
    midx = _mod_index(lay, TOKEN_ROWS)

    def act_spec(a):
        if a.ndim == 3:
            return pl.BlockSpec((a.shape[0], TOKEN_ROWS, a.shape[2]), lambda i: (0, i, 0))
        return pl.BlockSpec((TOKEN_ROWS, a.shape[1]), lambda i: (i, 0))

    const = lambda a: pl.BlockSpec(a.shape, lambda i: (0, 0))
    resident = lambda a: pl.BlockSpec(a.shape, lambda i: (0, 0), pipeline_mode=pl.Buffered(1))
    of_layer = lambda a: pl.BlockSpec((1,) + a.shape[1:], lambda i: (layer, 0, 0),
                                      pipeline_mode=pl.Buffered(1))
    return pl.pallas_call(
        body, name=name,
        grid=(rows // TOKEN_ROWS,),
        in_specs=[act_spec(a) for a in acts] + [
            act_spec(x), pl.BlockSpec((1, SEQ_GROUP, N_MOD * d), lambda i: (midx(i), 0, 0))]
            + [const(s) for s in small] + [resident(wout), const(gpre), const(gpost),
                                           of_layer(w1), of_layer(w2)],
        out_specs=act_spec(x),
        out_shape=jax.ShapeDtypeStruct((rows, d), F32),
        compiler_params=_params(),
    )(*acts, x, mod, *small, wout, gpre, gpost, w1, w2)


def _s5_discretise_kernel(lr_ref, li_ref, ldt_ref, bre_ref, bim_ref,
                          ar_ref, ai_ref, bbre_ref, bbim_ref):
    lr, li = lr_ref[...], li_ref[...]
    dt = jnp.exp(ldt_ref[...])
    mag = jnp.exp(lr * dt)
    ar = mag * jnp.cos(li * dt)
    ai = mag * jnp.sin(li * dt)
    den = lr * lr + li * li
    fr = ((ar - 1.0) * lr + ai * li) / den
    fi = (ai * lr - (ar - 1.0) * li) / den
    ar_ref[...] = ar
    ai_ref[...] = ai
    bbre_ref[...] = fr * bre_ref[...] - fi * bim_ref[...]
    bbim_ref[...] = fr * bim_ref[...] + fi * bre_ref[...]


def _s5_discretise(lam_re, lam_im, log_dt, b_re, b_im):
    e, nd, g, p = lam_re.shape
    c = b_re.shape[-1]
    full = (e, nd, g, c, p)
    flat = lambda a: jnp.broadcast_to(a, full).reshape(-1, p)
    lr = flat(lam_re[:, :, :, None, :])
    li = flat(lam_im[:, :, :, None, :])
    ldt = flat(log_dt[:, :, :, None, None])
    bre = flat(b_re.transpose(0, 1, 3, 2)[:, None])
    bim = flat(b_im.transpose(0, 1, 3, 2)[:, None])
    shape = jax.ShapeDtypeStruct(lr.shape, F32)
    ar, ai, bbre, bbim = pl.pallas_call(
        _s5_discretise_kernel, name="s5_discretise", out_shape=[shape] * 4,
        compiler_params=pltpu.CompilerParams(vmem_limit_bytes=VMEM_LIMIT_BYTES),
    )(lr, li, ldt, bre, bim)
    ar = ar.reshape(full)[:, :, :, 0, :]
    ai = ai.reshape(full)[:, :, :, 0, :]
    return ar, ai, bbre.reshape(full), bbim.reshape(full)


def _block_diag_in(bb):
    g, c, p = bb.shape
    per = S5_SLICE // c
    n = g // per
    blocks = bb.reshape(n, per, c, p)
    eye = jnp.eye(per, dtype=bb.dtype)
    return jnp.einsum('nicp,ij->nicjp', blocks, eye).reshape(n, per * c, per * p)


def _block_diag_out(cc):
    g, c, p = cc.shape
    per = S5_SLICE // c
    n = g // per
    blocks = cc.reshape(n, per, c, p)
    eye = jnp.eye(per, dtype=cc.dtype)
    return jnp.einsum('nicp,ij->nipjc', blocks, eye).reshape(n, per * p, per * c)


def _s5_scan_kernel(u_ref, h0re_ref, h0im_ref, a_ref, bw_ref, cw_ref,
                    y_ref, fre_ref, fim_ref, hre_ref, him_ref, bre_ref, bim_ref,
                    *, lay, reverse):
    _, grp, first, last, _, _ = _schedule(lay, SCAN_ROWS, reverse, pl.program_id(0))
    is_ctx = grp < lay.n_ctx
    n_slices = bw_ref.shape[0]
    wide = bw_ref.shape[2] // 2

    @pl.when(first)
    def _():
        hre_ref[...] = jnp.where(is_ctx, 0.0, h0re_ref[0])
        him_ref[...] = jnp.where(is_ctx, 0.0, h0im_ref[0])

    for j in range(n_slices):
        uj = u_ref[:, j * S5_SLICE:(j + 1) * S5_SLICE].astype(BF16)
        bb = _dot(uj, bw_ref[j])
        bre_ref[:, j * wide:(j + 1) * wide] = bb[:, :wide]
        bim_ref[:, j * wide:(j + 1) * wide] = bb[:, wide:]

    steps = range(SCAN_STEPS - 1, -1, -1) if reverse else range(SCAN_STEPS)
    for j in range(n_slices):
        cols = slice(j * wide, (j + 1) * wide)
        ar = a_ref[0, :, cols]
        ai = a_ref[1, :, cols]
        hr = hre_ref[:, cols]
        hi = him_ref[:, cols]
        for t in steps:
            rows = slice(t * SEQ_GROUP, (t + 1) * SEQ_GROUP)
            nr = ar * hr - ai * hi + bre_ref[rows, cols]
            ni = ar * hi + ai * hr + bim_ref[rows, cols]
            bre_ref[rows, cols] = nr
            bim_ref[rows, cols] = ni
            hr, hi = nr, ni
        hre_ref[:, cols] = hr
        him_ref[:, cols] = hi

    for j in range(n_slices):
        cols = slice(j * wide, (j + 1) * wide)
        yj = _dot(bre_ref[:, cols].astype(BF16), cw_ref[0, j])
        yj = yj - _dot(bim_ref[:, cols].astype(BF16), cw_ref[1, j])
        y_ref[:, j * S5_SLICE:(j + 1) * S5_SLICE] = yj

    @pl.when(last & is_ctx)
    def _():
        fre_ref[0] = hre_ref[...]
        fim_ref[0] = him_ref[...]


def _state_maps(lay, sched, ndim):
    pad = (0,) * (ndim - 1)
    start = lambda i: (jnp.maximum(sched(i)[1] - lay.n_ctx, 0),) + pad
    final = lambda i: (jnp.minimum(sched(i)[1], lay.n_ctx - 1),) + pad
    return start, final


def _s5_scan(lay, u, h0re, h0im, a8, bw, cw, reverse):
    rows, da = u.shape
    ns = h0re.shape[-1]
    sched = functools.partial(_schedule, lay, SCAN_ROWS, reverse)
    tile = lambda i: (sched(i)[0], 0)
    start, final = _state_maps(lay, sched, 3)
    state_shape = jax.ShapeDtypeStruct((lay.n_ctx, SEQ_GROUP, ns), F32)
    return pl.pallas_call(
        functools.partial(_s5_scan_kernel, lay=lay, reverse=reverse), name="s5_scan",
        grid=(lay.tiles(SCAN_ROWS),),
        in_specs=[pl.BlockSpec((SCAN_ROWS, da), tile),
                  pl.BlockSpec((1, SEQ_GROUP, ns), start),
                  pl.BlockSpec((1, SEQ_GROUP, ns), start),
                  pl.BlockSpec(a8.shape, lambda i: (0, 0, 0)),
                  pl.BlockSpec(bw.shape, lambda i: (0, 0, 0)),
                  pl.BlockSpec(cw.shape, lambda i: (0, 0, 0, 0))],
        out_specs=[pl.BlockSpec((SCAN_ROWS, da), tile),
                   pl.BlockSpec((1, SEQ_GROUP, ns), final),
                   pl.BlockSpec((1, SEQ_GROUP, ns), final)],
        out_shape=[jax.ShapeDtypeStruct((rows, da), F32), state_shape, state_shape],
        scratch_shapes=[pltpu.VMEM((SEQ_GROUP, ns), F32), pltpu.VMEM((SEQ_GROUP, ns), F32),
                        pltpu.VMEM((SCAN_ROWS, ns), F32), pltpu.VMEM((SCAN_ROWS, ns), F32)],
        compiler_params=_params(),
    )(u, h0re, h0im, a8, bw, cw)


def _halo_specs(lay, steps, channels, reverse=False, slabs=None):
    sched = functools.partial(_schedule, lay, steps * SEQ_GROUP, reverse)
    last_pair = lay.rows // (2 * SEQ_GROUP) - 1
    lead = () if slabs is None else (0,)
    shape = () if slabs is None else (slabs,)

    def prev(i):
        t = sched(i)[0]
        return lead + (jnp.maximum(t * steps - 1, 0), 0)

    def nxt(i):
        t = sched(i)[0]
        return lead + (jnp.minimum((t + 1) * (steps // 2), last_pair), 0)

    return (pl.BlockSpec(shape + (SEQ_GROUP, channels), prev),
            pl.BlockSpec(shape + (2 * SEQ_GROUP, channels), nxt))


def _conv_centred(x, prev, nxt, k, n, w, b):
    prev = jnp.where(k > 0, prev, 0.0)
    nxt = jnp.where(k < n - 1, nxt, 0.0)
    xp = jnp.concatenate([prev, x, nxt], axis=0)
    r = x.shape[0]
    out = b + xp[0:r] * w[0:1, :]
    for j in range(1, CONV_K):
        out = out + xp[j * SEQ_GROUP:j * SEQ_GROUP + r] * w[j:j + 1, :]
    return out


def _lru_scan_kernel(x_ref, h0_ref, wr_ref, wi_ref, br_ref, bi_ref, lam_ref,
                     h_ref, fin_ref, st_ref, a_ref, b_ref, *, lay, reverse):
    _, grp, first, last, _, _ = _schedule(lay, SCAN_ROWS, reverse, pl.program_id(0))
    is_ctx = grp < lay.n_ctx

    @pl.when(first)
    def _():
        st_ref[...] = jnp.where(is_ctx, 0.0, h0_ref[0])

    x = x_ref[...]
    xb = x.astype(BF16)
    bs = wr_ref.shape[-1]
    neg_sp = -LRU_C * _softplus(-lam_ref[...])
    for blk in range(LRU_BLOCKS):
        cols = slice(blk * bs, (blk + 1) * bs)
        r = _sigmoid(_dot(xb[:, cols], wr_ref[blk]) + br_ref[:, cols])
        gi = _sigmoid(_dot(xb[:, cols], wi_ref[blk]) + bi_ref[:, cols])
        log_a = neg_sp[:, cols] * r
        th = jnp.tanh(log_a)
        a_ref[:, cols] = jnp.exp(log_a)
        y = -2.0 * th / (1.0 - th)
        root = jnp.where(y > 0.0, y * lax.rsqrt(y), 0.0)
        b_ref[:, cols] = root * (gi * x[:, cols])

    steps = range(SCAN_STEPS - 1, -1, -1) if reverse else range(SCAN_STEPS)
    h = st_ref[...]
    for t in steps:
        rows = slice(t * SEQ_GROUP, (t + 1) * SEQ_GROUP)
        h = a_ref[rows, :] * h + b_ref[rows, :]
        h_ref[rows, :] = h
    st_ref[...] = h

    @pl.when(last & is_ctx)
    def _():
        fin_ref[0] = st_ref[...]


def _lru_scan(lay, xb, h0, wr, wi, br, bi, lam, reverse):
    rows, d = xb.shape
    sched = functools.partial(_schedule, lay, SCAN_ROWS, reverse)
    tile = lambda i: (sched(i)[0], 0)
    start, final = _state_maps(lay, sched, 3)
    const2 = lambda a: pl.BlockSpec(a.shape, lambda i: (0, 0))
    const3 = lambda a: pl.BlockSpec(a.shape, lambda i: (0, 0, 0))
    return pl.pallas_call(
        functools.partial(_lru_scan_kernel, lay=lay, reverse=reverse), name="lru_scan",
        grid=(lay.tiles(SCAN_ROWS),),
        in_specs=[pl.BlockSpec((SCAN_ROWS, d), tile),
                  pl.BlockSpec((1, SEQ_GROUP, d), start),
                  const3(wr), const3(wi), const2(br), const2(bi), const2(lam)],
        out_specs=[pl.BlockSpec((SCAN_ROWS, d), tile), pl.BlockSpec((1, SEQ_GROUP, d), final)],
        out_shape=[jax.ShapeDtypeStruct((rows, d), F32),
                   jax.ShapeDtypeStruct((lay.n_ctx, SEQ_GROUP, d), F32)],
        scratch_shapes=[pltpu.VMEM((SEQ_GROUP, d), F32),
                        pltpu.VMEM((SCAN_ROWS, d), F32), pltpu.VMEM((SCAN_ROWS, d), F32)],
        compiler_params=_params(),
    )(xb, h0, wr, wi, br, bi, lam)


def _unit_triangular_inverses(lmats, eye, blocks):
    c = lmats[0].shape[0]
    ts = [jnp.where(blocks[0], -l, 0.0) for l in lmats]
    tbs = [t.astype(BF16) for t in ts]
    invs = [eye + t for t in ts]
    ps = [_dot(tb, tb) for tb in tbs]
    ys = [_dot(jnp.concatenate([x, p], axis=0).astype(BF16), p.astype(BF16))
          for x, p in zip(invs, ps)]
    invs = [x + y[:c] for x, y in zip(invs, ys)]
    zs = [_dot(x.astype(BF16), y[c:].astype(BF16)) for x, y in zip(invs, ys)]
    invs = [x + z for x, z in zip(invs, zs)]
    inside = blocks[0]
    for outer in blocks[1:] + [None]:
        off = ~inside if outer is None else (outer & ~inside)
        es = [jnp.where(off, l, 0.0).astype(BF16) for l in lmats]
        xbs = [x.astype(BF16) for x in invs]
        xes = [_dot(xb, e).astype(BF16) for xb, e in zip(xbs, es)]
        xexs = [_dot(xe, xb) for xe, xb in zip(xes, xbs)]
        invs = [x - xex for x, xex in zip(invs, xexs)]
        inside = outer
    return invs


def _gdn_kernel(qkv_ref, gate_ref, s0_ref, o_ref, fin_ref, s_ref, *, lay, reverse, direction):
    _, grp, first, last, _, _ = _schedule(lay, GDN_TILE_ROWS, reverse, pl.program_id(0))
    is_ctx = grp < lay.n_ctx
    c = GDN_CHUNK

    @pl.when(first)
    def _():
        s_ref[...] = jnp.where(is_ctx, 0.0, s0_ref[0])

    ri = lax.broadcasted_iota(jnp.int32, (c, c), 0)
    ci = lax.broadcasted_iota(jnp.int32, (c, c), 1)
    incl = (ri <= ci) if reverse else (ri >= ci)
    strict = (ri < ci) if reverse else (ri > ci)
    tri = jnp.where(incl, 1.0, 0.0).astype(F32)
    eye = jnp.where(ri == ci, 1.0, 0.0).astype(F32)
    blocks = []
    shift = 3
    while (1 << shift) < c:
        blocks.append((ri >> shift) == (ci >> shift))
        shift += 1
    end_row = 0 if reverse else c - 1
    seq_rows = lambda b: pl.ds(b, c, stride=SEQ_GROUP)
    for b0 in range(0, SEQ_GROUP, GDN_SEQ_BATCH):
        _delta_rule_chunk(range(b0, b0 + GDN_SEQ_BATCH), qkv_ref, gate_ref, s_ref, o_ref, seq_rows,
                          (incl, strict, tri, eye, blocks), end_row, direction)

    @pl.when(last & is_ctx)
    def _():
        fin_ref[0] = s_ref[...]


def _delta_rule_chunk(seqs, qkv_ref, gate_ref, s_ref, o_ref, seq_rows, masks, end_row, direction):
    incl, strict, tri, eye, blocks = masks
    c = GDN_CHUNK
    dk = qkv_ref.shape[-1]
    keys = [(b, h) for b in seqs for h in range(GDN_HEADS)]

    gcols, grams, decays, kbs, qes, betas = {}, {}, {}, {}, {}, {}
    for b in seqs:
        gates = gate_ref[seq_rows(b), :]
        gc = _dot_hi(tri, gates)
        gc_t = gc.T
        for h in range(GDN_HEADS):
            lane = direction * GDN_HEADS + h
            q = qkv_ref[h, seq_rows(b), :]
            k = qkv_ref[GDN_HEADS + h, seq_rows(b), :]
            beta = gates[:, 2 * GDN_HEADS + lane:2 * GDN_HEADS + lane + 1]
            gcol = gc[:, lane:lane + 1]
            grow = gc_t[lane:lane + 1, :]
            egc = jnp.exp(gcol)
            gcols[b, h] = gcol
            betas[b, h] = beta
            decays[b, h] = jnp.exp(jnp.where(incl, gcol - grow, -1e30))
            kb = k * beta
            kbs[b, h] = kb * egc
            qes[b, h] = q * egc
            grams[b, h] = _dot_nt(jnp.concatenate([kb, q], axis=0).astype(BF16), k.astype(BF16))
    lmats = [jnp.where(strict, grams[key][:c] * decays[key], 0.0) for key in keys]
    invs = _unit_triangular_inverses(lmats, eye, blocks)
    uws = {}
    for key, inv in zip(keys, invs):
        b, h = key
        vb = qkv_ref[2 * GDN_HEADS + h, seq_rows(b), :] * betas[key]
        rhs = jnp.concatenate([vb, kbs[key]], axis=1)
        uws[key] = _dot(inv.astype(BF16), rhs.astype(BF16))

    ss = {key: s_ref[key[0], key[1]] for key in keys}
    projs = {key: _dot(jnp.concatenate([uws[key][:, dk:], qes[key]], axis=0).astype(BF16),
                       ss[key].astype(BF16)) for key in keys}
    v_news = {key: uws[key][:, :dk] - projs[key][:c] for key in keys}
    for key in keys:
        b, h = key
        k = qkv_ref[GDN_HEADS + h, seq_rows(b), :]
        g_end = gcols[key][end_row:end_row + 1, :]
        kd = k * jnp.exp(g_end - gcols[key])
        s_ref[b, h] = ss[key] * jnp.exp(g_end) + _dot_tn(kd.astype(BF16),
                                                         v_news[key].astype(BF16))
    for key in keys:
        b, h = key
        qk = jnp.where(incl, grams[key][c:] * decays[key], 0.0)
        o_ref[h, seq_rows(b), :] = projs[key][c:] + _dot(qk.astype(BF16),
                                                         v_news[key].astype(BF16))


def _gdn_scan(lay, qkv, gates, s0, reverse, direction):
    slabs, rows, dk = qkv.shape
    sched = functools.partial(_schedule, lay, GDN_TILE_ROWS, reverse)
    tile3 = lambda i: (0, sched(i)[0], 0)
    tile2 = lambda i: (sched(i)[0], 0)
    start, final = _state_maps(lay, sched, 5)
    sblock = (1,) + s0.shape[1:]
    qkv_spec = pl.BlockSpec((slabs, GDN_TILE_ROWS, dk), tile3)
    gate_spec = pl.BlockSpec((GDN_TILE_ROWS, LANES), tile2)
    out_specs = [pl.BlockSpec((GDN_HEADS, GDN_TILE_ROWS, dk), tile3), pl.BlockSpec(sblock, final)]
    out_shape = [jax.ShapeDtypeStruct((GDN_HEADS, rows, dk), F32),
                 jax.ShapeDtypeStruct((lay.n_ctx,) + s0.shape[1:], F32)]
    return pl.pallas_call(
        functools.partial(_gdn_kernel, lay=lay, reverse=reverse, direction=direction),
        name="gdn_scan", grid=(lay.tiles(GDN_TILE_ROWS),),
        in_specs=[qkv_spec, gate_spec, pl.BlockSpec(sblock, start)],
        out_specs=out_specs, out_shape=out_shape,
        scratch_shapes=[pltpu.VMEM(s0.shape[1:], F32)],
        compiler_params=_params(),
    )(qkv, gates, s0)


def _group_states(lat_state, flatten=True):
    db = lat_state.shape[0]
    tail = (-1,) if flatten else lat_state.shape[1:]
    return lat_state.reshape((db // SEQ_GROUP, SEQ_GROUP) + tail).astype(F32)


def kernel(x_prompt, x_sample, state_s5_re, state_s5_im, state_delta, state_lru, c, c_ctx, w_ada, b_ada, norm_mix_pre, norm_mix_post, norm_mlp_pre, norm_mlp_post, w_mlp_in, w_mlp_out, w_in_even, w_out_even, s5_lam_re, s5_lam_im, s5_log_dt, s5_b_re, s5_b_im, s5_c_re, s5_c_im, s5_d, gdn_conv_w, gdn_conv_b, gdn_a_log, gdn_dt_bias, gdn_o_norm, w_in_odd, w_out_odd, lru_conv_w, lru_conv_b, lru_w_r, lru_b_r, lru_w_i, lru_b_i, lru_lam):
    bp, tp, d = x_prompt.shape
    bl, tl, _ = x_sample.shape
    depth = w_ada.shape[0]
    n_dir = 2
    da = s5_d.shape[1]
    db = gdn_conv_w.shape[2] // 3
    heads = GDN_HEADS
    ngrp = s5_lam_re.shape[2]

    lay = Layout(bp // SEQ_GROUP, tp * SEQ_GROUP, bl // SEQ_GROUP, tl * SEQ_GROUP)

    cond = jnp.concatenate([c_ctx[None].astype(F32), c.astype(F32)], axis=0)
    pad = (-cond.shape[0]) % SUBLANES
    cond = jnp.pad(cond, ((0, pad), (0, 0)))
    ada = _ada_vectors(cond, w_ada, b_ada)
    mod_ctx = jnp.broadcast_to(ada[:, 0:1], (depth, SEQ_GROUP, ada.shape[-1]))[:, None]
    mod_lat = ada[:, 1:1 + bl].reshape(depth, bl // SEQ_GROUP, SEQ_GROUP, -1)
    mod = jnp.concatenate([mod_ctx, mod_lat], axis=1)

    x = _embed(lay, x_prompt.astype(F32), x_sample.astype(F32), _grid_sincos_table(tl, d))

    a_re, a_im, bb_re, bb_im = _s5_discretise(s5_lam_re, s5_lam_im, s5_log_dt, s5_b_re, s5_b_im)
    c_re_t = s5_c_re
    c_im_t = s5_c_im

    row = lambda v: v.reshape(1, -1).astype(F32)
    w_mlp_in_bf = w_mlp_in.astype(BF16)
    w_mlp_out_bf = w_mlp_out.astype(BF16)
    new_re, new_im, new_delta, new_lru = [], [], [], []
    for l in range(depth):
        mod_l = mod[l]
        if l % 2 == 0:
            e = l // 2
            n_in = w_in_even.shape[2]
            n_pad = (-n_in) % LANES
            w_in = jnp.pad(w_in_even[e], ((0, 0), (0, n_pad))).astype(BF16)
            splits = ((0, da, "plain"), (da, 2 * da, "plain"), (2 * da, 2 * da + 3 * db, "qkv"),
                      (2 * da + 3 * db, 2 * da + 4 * db, "plain"),
                      (2 * da + 4 * db, n_in + n_pad, "gates"))
            pad8 = lambda v: jnp.pad(v.reshape(1, -1).astype(F32), ((0, 0), (0, LANES - v.size)))
            n_slabs = 3 * db // LANES
            conv_w = gdn_conv_w[e].astype(F32).reshape(CONV_K, n_slabs, LANES).transpose(1, 0, 2)
            conv_b = gdn_conv_b[e].astype(F32).reshape(n_slabs, 1, LANES)
            u, za, qkv, zb, gates = _pre_mixer(
                lay, x, mod_l, row(norm_mix_pre[l]), w_in, splits,
                (conv_w, conv_b, pad8(gdn_a_log[e]), pad8(gdn_dt_bias[e])))

            cw = jnp.stack([_block_diag_out(c_re_t[e]), _block_diag_out(c_im_t[e])]).astype(BF16)
            ys, fr, fi = [], [], []
            for dd in range(n_dir):
                a8 = jnp.stack([a_re[e, dd].reshape(-1), a_im[e, dd].reshape(-1)])
                a8 = jnp.broadcast_to(a8[:, None, :], (2, SEQ_GROUP, a8.shape[-1]))
                bw = jnp.concatenate([_block_diag_in(bb_re[e, dd]), _block_diag_in(bb_im[e, dd])],
                                     axis=-1).astype(BF16)
                y, f_re, f_im = _s5_scan(lay, u, _group_states(state_s5_re[:, e, dd]),
                                         _group_states(state_s5_im[:, e, dd]),
                                         a8, bw, cw, reverse=(dd == 1))
                ys.append(y)
                fr.append(f_re.reshape(bp, ngrp, -1))
                fi.append(f_im.reshape(bp, ngrp, -1))
            new_re.append(jnp.stack(fr, axis=1))
            new_im.append(jnp.stack(fi, axis=1))

            os_, fd = [], []
            for dd in range(n_dir):
                s0 = _group_states(state_delta[:, e, dd], flatten=False)
                o_d, s_fin = _gdn_scan(lay, qkv, gates, s0, reverse=(dd == 1), direction=dd)
                os_.append(o_d)
                fd.append(s_fin.reshape((bp,) + s_fin.shape[2:]))
            new_delta.append(jnp.stack(fd, axis=1))

            acts = [u, za, ys[0], ys[1], os_[0], os_[1], zb]
            small = [row(s5_d[e]), row(gdn_o_norm[e]), row(norm_mix_post[l])]
            body, name, w_out = _tail_even_kernel, "tail_even", w_out_even[e]
        else:
            o = l // 2
            dr = lru_lam.shape[2]
            w_in = w_in_odd[o].astype(BF16)
            xb, yg = _pre_mixer(lay, x, mod_l, row(norm_mix_pre[l]), w_in,
                                ((0, dr, "conv"), (dr, 2 * dr, "plain")),
                                (lru_conv_w[o].astype(F32), row(lru_conv_b[o])))
            hs, fl = [], []
            for dd in range(n_dir):
                h, h_fin = _lru_scan(lay, xb, _group_states(state_lru[:, o, dd]),
                                     lru_w_r[o, dd].astype(BF16), lru_w_i[o, dd].astype(BF16),
                                     row(lru_b_r[o, dd]), row(lru_b_i[o, dd]), row(lru_lam[o, dd]),
                                     reverse=(dd == 1))
                hs.append(h)
                fl.append(h_fin.reshape(bp, dr))
            new_lru.append(jnp.stack(fl, axis=1))
            acts = [hs[0], hs[1], yg]
            small = [row(norm_mix_post[l])]
            body, name, w_out = _tail_odd_kernel, "tail_odd", w_out_odd[o]

        x = _layer_tail(lay, body, name, acts, x, mod_l, small, w_out.astype(BF16),
                        row(norm_mlp_pre[l]), row(norm_mlp_post[l]), w_mlp_in_bf, w_mlp_out_bf, l)

    y_prompt, y_sample = _unembed(lay, x, x_prompt.shape, x_sample.shape)
    y_prompt = y_prompt.astype(x_prompt.dtype)
    y_sample = y_sample.astype(x_sample.dtype)
    p_a = s5_lam_re.shape[3]
    new_s5_re = jnp.stack(new_re, axis=1).reshape(bp, -1, n_dir, ngrp, p_a)
    new_s5_im = jnp.stack(new_im, axis=1).reshape(bp, -1, n_dir, ngrp, p_a)
    return (y_prompt, y_sample, new_s5_re, new_s5_im,
            jnp.stack(new_delta, axis=1), jnp.stack(new_lru, axis=1))
```

```python
import functools
from typing import NamedTuple

import jax
import jax.numpy as jnp
from jax import lax
from jax.experimental import pallas as pl
from jax.experimental.pallas import tpu as pltpu

F32 = jnp.float32
BF16 = jnp.bfloat16
HIGHEST = lax.Precision.HIGHEST

EPS = 1e-6
POS_BASE = 10000.0
GRID_W = 64
N_MOD = 6
S5_GROUP = 16
S5_STATES = 64
GDN_HEADS = 4
GDN_CHUNK = 64
CONV_K = 4
CONV_LEFT = (CONV_K - 1) // 2
LRU_BLOCKS = 4
LRU_C = 8.0

SUBLANES = 8
LANES = 128
SEQ_GROUP = SUBLANES
SCAN_STEPS = 128
SCAN_ROWS = SCAN_STEPS * SEQ_GROUP
GDN_TILE_ROWS = GDN_CHUNK * SEQ_GROUP
GDN_SEQ_BATCH = 8
ADA_COLS = 1536
RELAYOUT_STEPS = 64
TOKEN_ROWS = 512
PRE_ROWS = 1024
MLP_HIDDEN_BLOCK = 2048
TAIL_ROW_BLOCK = 512
S5_SLICE = 128
VMEM_LIMIT_BYTES = 56 * 1024 * 1024


class Layout(NamedTuple):
    n_ctx: int
    len_ctx: int
    n_lat: int
    len_lat: int

    @property
    def ctx_rows(self):
        return self.n_ctx * self.len_ctx

    @property
    def rows(self):
        return self.ctx_rows + self.n_lat * self.len_lat

    @property
    def groups(self):
        return self.n_ctx + self.n_lat

    def tiles(self, tile_rows):
        return self.ctx_rows // tile_rows + self.n_lat * (self.len_lat // tile_rows)


def _schedule(lay, tile_rows, reverse, i):
    nc = lay.len_ctx // tile_rows
    nl = lay.len_lat // tile_rows
    n_ctx_tiles = lay.n_ctx * nc
    is_ctx = i < n_ctx_tiles
    j = jnp.maximum(i - n_ctx_tiles, 0)
    ic = jnp.minimum(i, n_ctx_tiles - 1)
    g = jnp.where(is_ctx, ic // nc, lay.n_ctx + j // nl)
    step = jnp.where(is_ctx, ic % nc, j % nl)
    n = jnp.where(is_ctx, nc, nl)
    k = (n - 1 - step) if reverse else step
    base = jnp.where(is_ctx, (ic // nc) * nc, n_ctx_tiles + (j // nl) * nl)
    return base + k, g, step == 0, step == n - 1, k, n


def _params(n_axes=1):
    return pltpu.CompilerParams(dimension_semantics=("arbitrary",) * n_axes,
                                vmem_limit_bytes=VMEM_LIMIT_BYTES)


def _dot(a, b):
    return jnp.dot(a, b, preferred_element_type=F32)


def _dot_hi(a, b):
    return jnp.dot(a, b, precision=HIGHEST, preferred_element_type=F32)


def _dot_nt(a, b):
    return lax.dot_general(a, b, (((1,), (1,)), ((), ())), preferred_element_type=F32)


def _dot_tn(a, b):
    return lax.dot_general(a, b, (((0,), (0,)), ((), ())), preferred_element_type=F32)


def _rms(x):
    return x * lax.rsqrt(jnp.mean(x * x, axis=-1, keepdims=True) + EPS)


def _per_group(rows_val, vec8):
    r, c = rows_val.shape
    return rows_val.reshape(r // SEQ_GROUP, SEQ_GROUP, c), vec8[None]


def _modulate(h, shift8, scale8):
    h3, sc = _per_group(h, scale8)
    out = h3 * (1.0 + sc) + shift8[None]
    return out.reshape(h.shape)


def _gated_residual(x, y, gate8):
    y3, g = _per_group(y, gate8)
    return x + (y3 * g).reshape(x.shape)


def _mod_chunk(mod_ref, idx, d):
    return mod_ref[0, :, idx * d:(idx + 1) * d]


def _softplus(x):
    return jnp.maximum(x, 0.0) + jnp.log1p(jnp.exp(-jnp.abs(x)))


def _sigmoid(x):
    return 0.5 * jnp.tanh(0.5 * x) + 0.5


def _ada_kernel(c_ref, w_ref, b_ref, o_ref):
    s = jax.nn.silu(c_ref[...]).astype(BF16)
    o_ref[0] = _dot(s, w_ref[0].astype(BF16)) + b_ref[0]


def _ada_vectors(cond, w_ada, b_ada):
    depth, d, n = w_ada.shape
    rows = cond.shape[0]
    tn = ADA_COLS
    return pl.pallas_call(
        _ada_kernel, name="ada_vectors",
        grid=(depth, n // tn),
        in_specs=[pl.BlockSpec((rows, d), lambda l, j: (0, 0)),
                  pl.BlockSpec((1, d, tn), lambda l, j: (l, 0, j)),
                  pl.BlockSpec((1, 1, tn), lambda l, j: (l, 0, j))],
        out_specs=pl.BlockSpec((1, rows, tn), lambda l, j: (l, 0, j)),
        out_shape=jax.ShapeDtypeStruct((depth, rows, n), F32),
        compiler_params=_params(2),
    )(cond, w_ada, b_ada.reshape(depth, 1, n))


def _family_maps(lay):
    sched = functools.partial(_schedule, lay, RELAYOUT_STEPS * SEQ_GROUP, False)
    nc = lay.len_ctx // (RELAYOUT_STEPS * SEQ_GROUP)

    def ctx(i):
        _, g, _, _, k, _ = sched(i)
        on = g < lay.n_ctx
        return (jnp.minimum(g, lay.n_ctx - 1), jnp.where(on, k, nc - 1), 0)

    def lat(i):
        _, g, _, _, k, _ = sched(i)
        on = g >= lay.n_ctx
        return (jnp.maximum(g - lay.n_ctx, 0), jnp.where(on, k, 0), 0)

    def lat_steps(i):
        return lat(i)[1:]

    return ctx, lat, lat_steps


def _embed_kernel(xc_ref, xl_ref, pos_ref, o_ref, slab_ref, *, lay):
    n_ctx_tiles = lay.ctx_rows // (RELAYOUT_STEPS * SEQ_GROUP)
    n_slabs = slab_ref.shape[0]

    def relayout(x_ref, pos):
        for b in range(SEQ_GROUP):
            xb = x_ref[b] if pos is None else x_ref[b] + pos
            for j in range(n_slabs):
                slab_ref[j, pl.ds(b, RELAYOUT_STEPS, stride=SEQ_GROUP), :] = (
                    xb[:, j * LANES:(j + 1) * LANES])
        for j in range(n_slabs):
            o_ref[:, j * LANES:(j + 1) * LANES] = slab_ref[j]

    @pl.when(pl.program_id(0) < n_ctx_tiles)
    def _():
        relayout(xc_ref, None)

    @pl.when(pl.program_id(0) >= n_ctx_tiles)
    def _():
        relayout(xl_ref, pos_ref[...])


def _embed(lay, x_ctx, x_lat, table):
    d = x_ctx.shape[-1]
    rows_tile = RELAYOUT_STEPS * SEQ_GROUP
    ctx, lat, lat_steps = _family_maps(lay)
    block = (SEQ_GROUP, RELAYOUT_STEPS, d)
    return pl.pallas_call(
        functools.partial(_embed_kernel, lay=lay), name="embed",
        grid=(lay.tiles(rows_tile),),
        in_specs=[pl.BlockSpec(block, ctx), pl.BlockSpec(block, lat),
                  pl.BlockSpec((RELAYOUT_STEPS, d), lat_steps)],
        out_specs=pl.BlockSpec((rows_tile, d), lambda i: (i, 0)),
        out_shape=jax.ShapeDtypeStruct((lay.rows, d), F32),
        scratch_shapes=[pltpu.VMEM((d // LANES, rows_tile, LANES), F32)],
        compiler_params=_params(),
    )(x_ctx, x_lat, table)


def _unembed_kernel(x_ref, yc_ref, yl_ref, slab_ref, *, lay):
    n_ctx_tiles = lay.ctx_rows // (RELAYOUT_STEPS * SEQ_GROUP)
    n_slabs = slab_ref.shape[0]
    for j in range(n_slabs):
        slab_ref[j] = x_ref[:, j * LANES:(j + 1) * LANES]

    def relayout(y_ref):
        for b in range(SEQ_GROUP):
            for j in range(n_slabs):
                y_ref[b, :, j * LANES:(j + 1) * LANES] = (
                    slab_ref[j, pl.ds(b, RELAYOUT_STEPS, stride=SEQ_GROUP), :])

    @pl.when(pl.program_id(0) < n_ctx_tiles)
    def _():
        relayout(yc_ref)

    @pl.when(pl.program_id(0) >= n_ctx_tiles)
    def _():
        relayout(yl_ref)


def _unembed(lay, x, shape_ctx, shape_lat):
    d = x.shape[-1]
    rows_tile = RELAYOUT_STEPS * SEQ_GROUP
    ctx, lat, _ = _family_maps(lay)
    block = (SEQ_GROUP, RELAYOUT_STEPS, d)
    return pl.pallas_call(
        functools.partial(_unembed_kernel, lay=lay), name="unembed",
        grid=(lay.tiles(rows_tile),),
        in_specs=[pl.BlockSpec((rows_tile, d), lambda i: (i, 0))],
        out_specs=[pl.BlockSpec(block, ctx), pl.BlockSpec(block, lat)],
        out_shape=[jax.ShapeDtypeStruct(shape_ctx, F32), jax.ShapeDtypeStruct(shape_lat, F32)],
        scratch_shapes=[pltpu.VMEM((d // LANES, rows_tile, LANES), F32)],
        compiler_params=_params(),
    )(x)


def _grid_sincos_table(n_tokens, d_model):
    rows = n_tokens // GRID_W
    row = jnp.repeat(jnp.arange(rows, dtype=F32), GRID_W)
    col = jnp.tile(jnp.arange(GRID_W, dtype=F32), rows)
    n_freq = d_model // 4
    omega = POS_BASE ** (-jnp.arange(n_freq, dtype=F32) / n_freq)
    ar = row[:, None] * omega
    ac = col[:, None] * omega
    return jnp.concatenate([jnp.sin(ar), jnp.cos(ar), jnp.sin(ac), jnp.cos(ac)], axis=-1)


def _mod_index(lay, tile_rows):
    def index(i):
        row = i * tile_rows
        lat = jnp.maximum(row - lay.ctx_rows, 0) // lay.len_lat
        return jnp.where(row < lay.ctx_rows, 0, 1 + lat)
    return index


def _pre_kernel(x_ref, xprev_ref, xnext_ref, mod_ref, gain_ref, w_ref, *refs, splits, lay):
    n_par = 2 * sum(kind != "plain" for _, _, kind in splits)
    par_refs, out_refs = refs[:n_par], refs[n_par:]
    _, _, _, _, kpos, npos = _schedule(lay, PRE_ROWS, False, pl.program_id(0))
    r, d = x_ref.shape
    lead = xnext_ref.shape[0]
    xall = jnp.concatenate([xnext_ref[...], x_ref[...], xprev_ref[...],
                            jnp.zeros_like(xprev_ref[...])], axis=0)
    h = _rms(xall) * gain_ref[...]
    h = _modulate(h, _mod_chunk(mod_ref, 0, d), _mod_chunk(mod_ref, 1, d)).astype(BF16)
    hmain = h[lead:lead + r]
    p = 0
    for (a, b, kind), o_ref in zip(splits, out_refs):
        if kind == "plain":
            o_ref[...] = _dot(hmain, w_ref[:, a:b])
            continue
        p0_ref, p1_ref = par_refs[p:p + 2]
        p += 2
        if kind == "gates":
            ab = _dot(hmain, w_ref[:, a:b])
            g_all = -jnp.exp(p0_ref[...]) * _softplus(ab + p1_ref[...])
            lane_id = lax.broadcasted_iota(jnp.int32, ab.shape, 1)
            o_ref[...] = jnp.where(lane_id < 2 * GDN_HEADS, g_all, _sigmoid(ab))
            continue
        acc = _dot(h, w_ref[:, a:b])
        nxt, main, prev = acc[0:lead], acc[lead:lead + r], acc[lead + r:lead + r + SEQ_GROUP]
        if kind == "conv":
            o_ref[...] = _conv_centred(main, prev, nxt, kpos, npos, p0_ref[...], p1_ref[...])
            continue
        dk = LANES
        for j in range((b - a) // dk):
            cols = slice(j * dk, (j + 1) * dk)
            x = _conv_centred(main[:, cols], prev[:, cols], nxt[:, cols], kpos, npos,
                              p0_ref[j], p1_ref[j])
            x = x * _sigmoid(x)
            if j < 2 * GDN_HEADS:
                x = x * lax.rsqrt(jnp.sum(x * x, axis=-1, keepdims=True) + EPS)
            if j < GDN_HEADS:
                x = x * (dk ** -0.5)
            o_ref[j] = x


def _pre_mixer(lay, x, mod, gain, w, splits, params):
    rows, d = x.shape
    n = w.shape[1]
    midx = _mod_index(lay, PRE_ROWS)
    prev_spec, next_spec = _halo_specs(lay, PRE_ROWS // SEQ_GROUP, d)
    out_specs, out_shapes = [], []
    for a, b, kind in splits:
        if kind == "qkv":
            ns = (b - a) // LANES
            out_specs.append(pl.BlockSpec((ns, PRE_ROWS, LANES), lambda i: (0, i, 0)))
            out_shapes.append(jax.ShapeDtypeStruct((ns, rows, LANES), F32))
        else:
            out_specs.append(pl.BlockSpec((PRE_ROWS, b - a), lambda i: (i, 0)))
            out_shapes.append(jax.ShapeDtypeStruct((rows, b - a), F32))
    const = lambda a: pl.BlockSpec(a.shape, lambda i: (0,) * a.ndim)
    return pl.pallas_call(
        functools.partial(_pre_kernel, splits=splits, lay=lay), name="pre_mixer",
        grid=(rows // PRE_ROWS,),
        in_specs=[pl.BlockSpec((PRE_ROWS, d), lambda i: (i, 0)), prev_spec, next_spec,
                  pl.BlockSpec((1, SEQ_GROUP, N_MOD * d), lambda i: (midx(i), 0, 0)),
                  pl.BlockSpec((1, d), lambda i: (0, 0)),
                  pl.BlockSpec((d, n), lambda i: (0, 0), pipeline_mode=pl.Buffered(1))]
                 + [const(a) for a in params],
        out_specs=out_specs, out_shape=out_shapes,
        compiler_params=_params(),
    )(x, x, x, mod, gain, w, *params)


def _finish_sublayer(x, out, gain, gate8):
    return _gated_residual(x, _rms(out) * gain, gate8)


def _mlp_sublayer(x, mod_ref, gpre_ref, gpost_ref, w1_ref, w2_ref):
    d = x.shape[-1]
    dff = w1_ref.shape[-1]
    h = _rms(x) * gpre_ref[...]
    h = _modulate(h, _mod_chunk(mod_ref, 3, d), _mod_chunk(mod_ref, 4, d)).astype(BF16)
    out = None
    for a in range(0, dff, MLP_HIDDEN_BLOCK):
        f = jnp.square(jnp.maximum(_dot(h, w1_ref[0, :, a:a + MLP_HIDDEN_BLOCK]), 0.0))
        part = _dot(f.astype(BF16), w2_ref[0, a:a + MLP_HIDDEN_BLOCK, :])
        out = part if out is None else out + part
    return _finish_sublayer(x, out, gpost_ref[...], _mod_chunk(mod_ref, 5, d))


def _tail_odd_kernel(hf_ref, hb_ref, yg_ref, x_ref, mod_ref, gmix_ref, wout_ref,
                     gpre_ref, gpost_ref, w1_ref, w2_ref, o_ref):
    d = x_ref.shape[-1]
    for r0 in range(0, x_ref.shape[0], TAIL_ROW_BLOCK):
        rs = slice(r0, r0 + TAIL_ROW_BLOCK)
        y = (hf_ref[rs, :] + hb_ref[rs, :]) * jax.nn.gelu(yg_ref[rs, :])
        out = _dot(y.astype(BF16), wout_ref[...])
        x = _finish_sublayer(x_ref[rs, :], out, gmix_ref[...], _mod_chunk(mod_ref, 2, d))
        o_ref[rs, :] = _mlp_sublayer(x, mod_ref, gpre_ref, gpost_ref, w1_ref, w2_ref)


def _tail_even_kernel(u_ref, za_ref, yf_ref, yb_ref, of_ref, ob_ref, zb_ref, x_ref, mod_ref,
                      dskip_ref, onorm_ref, gmix_ref, wout_ref,
                      gpre_ref, gpost_ref, w1_ref, w2_ref, o_ref):
    d = x_ref.shape[-1]
    da = u_ref.shape[-1]
    heads, _, dv = of_ref.shape
    for r0 in range(0, x_ref.shape[0], TAIL_ROW_BLOCK):
        rs = slice(r0, r0 + TAIL_ROW_BLOCK)
        ya = jax.nn.gelu(yf_ref[rs, :] + yb_ref[rs, :] + u_ref[rs, :] * dskip_ref[...])
        ya = ya * _sigmoid(za_ref[rs, :])
        out = _dot(ya.astype(BF16), wout_ref[0:da, :])
        for h in range(heads):
            o = of_ref[h, rs, :] + ob_ref[h, rs, :]
            zb = zb_ref[rs, h * dv:(h + 1) * dv]
            oh = _rms(o) * onorm_ref[...] * (zb * _sigmoid(zb))
            out = out + _dot(oh.astype(BF16), wout_ref[da + h * dv:da + (h + 1) * dv, :])
        x = _finish_sublayer(x_ref[rs, :], out, gmix_ref[...], _mod_chunk(mod_ref, 2, d))
        o_ref[rs, :] = _mlp_sublayer(x, mod_ref, gpre_ref, gpost_ref, w1_ref, w2_ref)


def _layer_tail(lay, body, name, acts, x, mod, small, wout, gpre, gpost, w1, w2, layer):
    rows, d = x.shape
    midx = _mod_index(lay, TOKEN_ROWS)

    def act_spec(a):
        if a.ndim == 3:
            return pl.BlockSpec((a.shape[0], TOKEN_ROWS, a.shape[2]), lambda i: (0, i, 0))
        return pl.BlockSpec((TOKEN_ROWS, a.shape[1]), lambda i: (i, 0))

    const = lambda a: pl.BlockSpec(a.shape, lambda i: (0, 0))
    resident = lambda a: pl.BlockSpec(a.shape, lambda i: (0, 0), pipeline_mode=pl.Buffered(1))
    of_layer = lambda a: pl.BlockSpec((1,) + a.shape[1:], lambda i: (layer, 0, 0),
                                      pipeline_mode=pl.Buffered(1))
    return pl.pallas_call(
        body, name=name,
        grid=(rows // TOKEN_ROWS,),
        in_specs=[act_spec(a) for a in acts] + [
            act_spec(x), pl.BlockSpec((1, SEQ_GROUP, N_MOD * d), lambda i: (midx(i), 0, 0))]
            + [const(s) for s in small] + [resident(wout), const(gpre), const(gpost),
                                           of_layer(w1), of_layer(w2)],
        out_specs=act_spec(x),
        out_shape=jax.ShapeDtypeStruct((rows, d), F32),
        compiler_params=_params(),
    )(*acts, x, mod, *small, wout, gpre, gpost, w1, w2)


def _s5_discretise_kernel(lr_ref, li_ref, ldt_ref, bre_ref, bim_ref,
                          ar_ref, ai_ref, bbre_ref, bbim_ref):
    lr, li = lr_ref[...], li_ref[...]
    dt = jnp.exp(ldt_ref[...])
    mag = jnp.exp(lr * dt)
    ar = mag * jnp.cos(li * dt)
    ai = mag * jnp.sin(li * dt)
    den = lr * lr + li * li
    fr = ((ar - 1.0) * lr + ai * li) / den
    fi = (ai * lr - (ar - 1.0) * li) / den
    ar_ref[...] = ar
    ai_ref[...] = ai
    bbre_ref[...] = fr * bre_ref[...] - fi * bim_ref[...]
    bbim_ref[...] = fr * bim_ref[...] + fi * bre_ref[...]


def _s5_discretise(lam_re, lam_im, log_dt, b_re, b_im):
    e, nd, g, p = lam_re.shape
    c = b_re.shape[-1]
    full = (e, nd, g, c, p)
    flat = lambda a: jnp.broadcast_to(a, full).reshape(-1, p)
    lr = flat(lam_re[:, :, :, None, :])
    li = flat(lam_im[:, :, :, None, :])
    ldt = flat(log_dt[:, :, :, None, None])
    bre = flat(b_re.transpose(0, 1, 3, 2)[:, None])
    bim = flat(b_im.transpose(0, 1, 3, 2)[:, None])
    shape = jax.ShapeDtypeStruct(lr.shape, F32)
    ar, ai, bbre, bbim = pl.pallas_call(
        _s5_discretise_kernel, name="s5_discretise", out_shape=[shape] * 4,
        compiler_params=pltpu.CompilerParams(vmem_limit_bytes=VMEM_LIMIT_BYTES),
    )(lr, li, ldt, bre, bim)
    ar = ar.reshape(full)[:, :, :, 0, :]
    ai = ai.reshape(full)[:, :, :, 0, :]
    return ar, ai, bbre.reshape(full), bbim.reshape(full)


def _block_diag_in(bb):
    g, c, p = bb.shape
    per = S5_SLICE // c
    n = g // per
    blocks = bb.reshape(n, per, c, p)
    eye = jnp.eye(per, dtype=bb.dtype)
    return jnp.einsum('nicp,ij->nicjp', blocks, eye).reshape(n, per * c, per * p)


def _block_diag_out(cc):
    g, c, p = cc.shape
    per = S5_SLICE // c
    n = g // per
    blocks = cc.reshape(n, per, c, p)
    eye = jnp.eye(per, dtype=cc.dtype)
    return jnp.einsum('nicp,ij->nipjc', blocks, eye).reshape(n, per * p, per * c)


def _s5_scan_kernel(u_ref, h0re_ref, h0im_ref, a_ref, bw_ref, cw_ref,
                    y_ref, fre_ref, fim_ref, hre_ref, him_ref, bre_ref, bim_ref,
                    *, lay, reverse):
    _, grp, first, last, _, _ = _schedule(lay, SCAN_ROWS, reverse, pl.program_id(0))
    is_ctx = grp < lay.n_ctx
    n_slices = bw_ref.shape[0]
    wide = bw_ref.shape[2] // 2

    @pl.when(first)
    def _():
        hre_ref[...] = jnp.where(is_ctx, 0.0, h0re_ref[0])
        him_ref[...] = jnp.where(is_ctx, 0.0, h0im_ref[0])

    for j in range(n_slices):
        uj = u_ref[:, j * S5_SLICE:(j + 1) * S5_SLICE].astype(BF16)
        bb = _dot(uj, bw_ref[j])
        bre_ref[:, j * wide:(j + 1) * wide] = bb[:, :wide]
        bim_ref[:, j * wide:(j + 1) * wide] = bb[:, wide:]

    steps = range(SCAN_STEPS - 1, -1, -1) if reverse else range(SCAN_STEPS)
    for j in range(n_slices):
        cols = slice(j * wide, (j + 1) * wide)
        ar = a_ref[0, :, cols]
        ai = a_ref[1, :, cols]
        hr = hre_ref[:, cols]
        hi = him_ref[:, cols]
        for t in steps:
            rows = slice(t * SEQ_GROUP, (t + 1) * SEQ_GROUP)
            nr = ar * hr - ai * hi + bre_ref[rows, cols]
            ni = ar * hi + ai * hr + bim_ref[rows, cols]
            bre_ref[rows, cols] = nr
            bim_ref[rows, cols] = ni
            hr, hi = nr, ni
        hre_ref[:, cols] = hr
        him_ref[:, cols] = hi

    for j in range(n_slices):
        cols = slice(j * wide, (j + 1) * wide)
        yj = _dot(bre_ref[:, cols].astype(BF16), cw_ref[0, j])
        yj = yj - _dot(bim_ref[:, cols].astype(BF16), cw_ref[1, j])
        y_ref[:, j * S5_SLICE:(j + 1) * S5_SLICE] = yj

    @pl.when(last & is_ctx)
    def _():
        fre_ref[0] = hre_ref[...]
        fim_ref[0] = him_ref[...]


def _state_maps(lay, sched, ndim):
    pad = (0,) * (ndim - 1)
    start = lambda i: (jnp.maximum(sched(i)[1] - lay.n_ctx, 0),) + pad
    final = lambda i: (jnp.minimum(sched(i)[1], lay.n_ctx - 1),) + pad
    return start, final


def _s5_scan(lay, u, h0re, h0im, a8, bw, cw, reverse):
    rows, da = u.shape
    ns = h0re.shape[-1]
    sched = functools.partial(_schedule, lay, SCAN_ROWS, reverse)
    tile = lambda i: (sched(i)[0], 0)
    start, final = _state_maps(lay, sched, 3)
    state_shape = jax.ShapeDtypeStruct((lay.n_ctx, SEQ_GROUP, ns), F32)
    return pl.pallas_call(
        functools.partial(_s5_scan_kernel, lay=lay, reverse=reverse), name="s5_scan",
        grid=(lay.tiles(SCAN_ROWS),),
        in_specs=[pl.BlockSpec((SCAN_ROWS, da), tile),
                  pl.BlockSpec((1, SEQ_GROUP, ns), start),
                  pl.BlockSpec((1, SEQ_GROUP, ns), start),
                  pl.BlockSpec(a8.shape, lambda i: (0, 0, 0)),
                  pl.BlockSpec(bw.shape, lambda i: (0, 0, 0)),
                  pl.BlockSpec(cw.shape, lambda i: (0, 0, 0, 0))],
        out_specs=[pl.BlockSpec((SCAN_ROWS, da), tile),
                   pl.BlockSpec((1, SEQ_GROUP, ns), final),
                   pl.BlockSpec((1, SEQ_GROUP, ns), final)],
        out_shape=[jax.ShapeDtypeStruct((rows, da), F32), state_shape, state_shape],
        scratch_shapes=[pltpu.VMEM((SEQ_GROUP, ns), F32), pltpu.VMEM((SEQ_GROUP, ns), F32),
                        pltpu.VMEM((SCAN_ROWS, ns), F32), pltpu.VMEM((SCAN_ROWS, ns), F32)],
        compiler_params=_params(),
    )(u, h0re, h0im, a8, bw, cw)


def _halo_specs(lay, steps, channels, reverse=False, slabs=None):
    sched = functools.partial(_schedule, lay, steps * SEQ_GROUP, reverse)
    last_pair = lay.rows // (2 * SEQ_GROUP) - 1
    lead = () if slabs is None else (0,)
    shape = () if slabs is None else (slabs,)

    def prev(i):
        t = sched(i)[0]
        return lead + (jnp.maximum(t * steps - 1, 0), 0)

    def nxt(i):
        t = sched(i)[0]
        return lead + (jnp.minimum((t + 1) * (steps // 2), last_pair), 0)

    return (pl.BlockSpec(shape + (SEQ_GROUP, channels), prev),
            pl.BlockSpec(shape + (2 * SEQ_GROUP, channels), nxt))


def _conv_centred(x, prev, nxt, k, n, w, b):
    prev = jnp.where(k > 0, prev, 0.0)
    nxt = jnp.where(k < n - 1, nxt, 0.0)
    xp = jnp.concatenate([prev, x, nxt], axis=0)
    r = x.shape[0]
    out = b + xp[0:r] * w[0:1, :]
    for j in range(1, CONV_K):
        out = out + xp[j * SEQ_GROUP:j * SEQ_GROUP + r] * w[j:j + 1, :]
    return out


def _lru_scan_kernel(x_ref, h0_ref, wr_ref, wi_ref, br_ref, bi_ref, lam_ref,
                     h_ref, fin_ref, st_ref, a_ref, b_ref, *, lay, reverse):
    _, grp, first, last, _, _ = _schedule(lay, SCAN_ROWS, reverse, pl.program_id(0))
    is_ctx = grp < lay.n_ctx

    @pl.when(first)
    def _():
        st_ref[...] = jnp.where(is_ctx, 0.0, h0_ref[0])

    x = x_ref[...]
    xb = x.astype(BF16)
    bs = wr_ref.shape[-1]
    neg_sp = -LRU_C * _softplus(-lam_ref[...])
    for blk in range(LRU_BLOCKS):
        cols = slice(blk * bs, (blk + 1) * bs)
        r = _sigmoid(_dot(xb[:, cols], wr_ref[blk]) + br_ref[:, cols])
        gi = _sigmoid(_dot(xb[:, cols], wi_ref[blk]) + bi_ref[:, cols])
        log_a = neg_sp[:, cols] * r
        th = jnp.tanh(log_a)
        a_ref[:, cols] = jnp.exp(log_a)
        y = -2.0 * th / (1.0 - th)
        root = jnp.where(y > 0.0, y * lax.rsqrt(y), 0.0)
        b_ref[:, cols] = root * (gi * x[:, cols])

    steps = range(SCAN_STEPS - 1, -1, -1) if reverse else range(SCAN_STEPS)
    h = st_ref[...]
    for t in steps:
        rows = slice(t * SEQ_GROUP, (t + 1) * SEQ_GROUP)
        h = a_ref[rows, :] * h + b_ref[rows, :]
        h_ref[rows, :] = h
    st_ref[...] = h

    @pl.when(last & is_ctx)
    def _():
        fin_ref[0] = st_ref[...]


def _lru_scan(lay, xb, h0, wr, wi, br, bi, lam, reverse):
    rows, d = xb.shape
    sched = functools.partial(_schedule, lay, SCAN_ROWS, reverse)
    tile = lambda i: (sched(i)[0], 0)
    start, final = _state_maps(lay, sched, 3)
    const2 = lambda a: pl.BlockSpec(a.shape, lambda i: (0, 0))
    const3 = lambda a: pl.BlockSpec(a.shape, lambda i: (0, 0, 0))
    return pl.pallas_call(
        functools.partial(_lru_scan_kernel, lay=lay, reverse=reverse), name="lru_scan",
        grid=(lay.tiles(SCAN_ROWS),),
        in_specs=[pl.BlockSpec((SCAN_ROWS, d), tile),
                  pl.BlockSpec((1, SEQ_GROUP, d), start),
                  const3(wr), const3(wi), const2(br), const2(bi), const2(lam)],
        out_specs=[pl.BlockSpec((SCAN_ROWS, d), tile), pl.BlockSpec((1, SEQ_GROUP, d), final)],
        out_shape=[jax.ShapeDtypeStruct((rows, d), F32),
                   jax.ShapeDtypeStruct((lay.n_ctx, SEQ_GROUP, d), F32)],
        scratch_shapes=[pltpu.VMEM((SEQ_GROUP, d), F32),
                        pltpu.VMEM((SCAN_ROWS, d), F32), pltpu.VMEM((SCAN_ROWS, d), F32)],
        compiler_params=_params(),
    )(xb, h0, wr, wi, br, bi, lam)


def _unit_triangular_inverses(lmats, eye, blocks):
    c = lmats[0].shape[0]
    zero = jnp.zeros((), lmats[0].dtype)
    tbs = [jnp.where(blocks[0], -l, zero) for l in lmats]
    invs = [eye + tb.astype(F32) for tb in tbs]
    ps = [_dot(tb, tb) for tb in tbs]
    ys = [_dot(jnp.concatenate([x, p], axis=0).astype(BF16), p.astype(BF16))
          for x, p in zip(invs, ps)]
    invs = [x + y[:c] for x, y in zip(invs, ys)]
    zs = [_dot(x.astype(BF16), y[c:].astype(BF16)) for x, y in zip(invs, ys)]
    invs = [x + z for x, z in zip(invs, zs)]
    inside = blocks[0]
    for outer in blocks[1:] + [None]:
        off = ~inside if outer is None else (outer & ~inside)
        es = [jnp.where(off, l, zero) for l in lmats]
        xbs = [x.astype(BF16) for x in invs]
        xes = [_dot(xb, e).astype(BF16) for xb, e in zip(xbs, es)]
        xexs = [_dot(xe, xb) for xe, xb in zip(xes, xbs)]
        invs = [x - xex for x, xex in zip(invs, xexs)]
        inside = outer
    return invs


def _gdn_kernel(qkv_ref, gate_ref, s0_ref, o_ref, fin_ref, s_ref, *, lay, reverse, direction):
    _, grp, first, last, _, _ = _schedule(lay, GDN_TILE_ROWS, reverse, pl.program_id(0))
    is_ctx = grp < lay.n_ctx
    c = GDN_CHUNK

    @pl.when(first)
    def _():
        s_ref[...] = jnp.where(is_ctx, 0.0, s0_ref[0])

    ri = lax.broadcasted_iota(jnp.int32, (c, c), 0)
    ci = lax.broadcasted_iota(jnp.int32, (c, c), 1)
    incl = (ri <= ci) if reverse else (ri >= ci)
    strict = (ri < ci) if reverse else (ri > ci)
    tri = jnp.where(incl, 1.0, 0.0).astype(F32)
    eye = jnp.where(ri == ci, 1.0, 0.0).astype(F32)
    blocks = []
    shift = 3
    while (1 << shift) < c:
        blocks.append((ri >> shift) == (ci >> shift))
        shift += 1
    end_row = 0 if reverse else c - 1
    seq_rows = lambda b: pl.ds(b, c, stride=SEQ_GROUP)
    for b0 in range(0, SEQ_GROUP, GDN_SEQ_BATCH):
        _delta_rule_chunk(range(b0, b0 + GDN_SEQ_BATCH), qkv_ref, gate_ref, s_ref, o_ref, seq_rows,
                          (incl, strict, tri, eye, blocks), end_row, direction)

    @pl.when(last & is_ctx)
    def _():
        fin_ref[0] = s_ref[...]


def _delta_rule_chunk(seqs, qkv_ref, gate_ref, s_ref, o_ref, seq_rows, masks, end_row, direction):
    incl, strict, tri, eye, blocks = masks
    c = GDN_CHUNK
    dk = qkv_ref.shape[-1]
    keys = [(b, h) for b in seqs for h in range(GDN_HEADS)]

    lmats, qks, rhss, qes, kds, s_decays = [], {}, {}, {}, {}, {}
    for b in seqs:
        gates = gate_ref[seq_rows(b), :]
        gc = _dot_hi(tri, gates)
        gc_t = gc.T
        for h in range(GDN_HEADS):
            lane = direction * GDN_HEADS + h
            q = qkv_ref[h, seq_rows(b), :]
            k = qkv_ref[GDN_HEADS + h, seq_rows(b), :]
            v = qkv_ref[2 * GDN_HEADS + h, seq_rows(b), :]
            beta = gates[:, 2 * GDN_HEADS + lane:2 * GDN_HEADS + lane + 1]
            gcol = gc[:, lane:lane + 1]
            grow = gc_t[lane:lane + 1, :]
            egc = jnp.exp(gcol)
            g_end = gcol[end_row:end_row + 1, :]
            decay = jnp.exp(jnp.where(incl, gcol - grow, -1e30))
            kb = k * beta
            gram = _dot_nt(jnp.concatenate([kb, q], axis=0).astype(BF16), k.astype(BF16))
            lmats.append(jnp.where(strict, gram[:c] * decay, 0.0).astype(BF16))
            qks[b, h] = jnp.where(incl, gram[c:] * decay, 0.0).astype(BF16)
            rhss[b, h] = jnp.concatenate([v * beta, kb * egc], axis=1).astype(BF16)
            qes[b, h] = (q * egc).astype(BF16)
            kds[b, h] = (k * jnp.exp(g_end - gcol)).astype(BF16)
            s_decays[b, h] = jnp.exp(g_end)
    invs = _unit_triangular_inverses(lmats, eye, blocks)
    uws = {key: _dot(inv.astype(BF16), rhss[key]) for key, inv in zip(keys, invs)}

    ss = {key: s_ref[key[0], key[1]] for key in keys}
    projs = {key: _dot(jnp.concatenate([uws[key][:, dk:].astype(BF16), qes[key]], axis=0),
                       ss[key].astype(BF16)) for key in keys}
    v_news = {key: (uws[key][:, :dk] - projs[key][:c]).astype(BF16) for key in keys}
    for key in keys:
        b, h = key
        s_ref[b, h] = ss[key] * s_decays[key] + _dot_tn(kds[key], v_news[key])
    for key in keys:
        b, h = key
        o_ref[h, seq_rows(b), :] = projs[key][c:] + _dot(qks[key], v_news[key])


def _gdn_scan(lay, qkv, gates, s0, reverse, direction):
    slabs, rows, dk = qkv.shape
    sched = functools.partial(_schedule, lay, GDN_TILE_ROWS, reverse)
    tile3 = lambda i: (0, sched(i)[0], 0)
    tile2 = lambda i: (sched(i)[0], 0)
    start, final = _state_maps(lay, sched, 5)
    sblock = (1,) + s0.shape[1:]
    qkv_spec = pl.BlockSpec((slabs, GDN_TILE_ROWS, dk), tile3)
    gate_spec = pl.BlockSpec((GDN_TILE_ROWS, LANES), tile2)
    out_specs = [pl.BlockSpec((GDN_HEADS, GDN_TILE_ROWS, dk), tile3), pl.BlockSpec(sblock, final)]
    out_shape = [jax.ShapeDtypeStruct((GDN_HEADS, rows, dk), F32),
                 jax.ShapeDtypeStruct((lay.n_ctx,) + s0.shape[1:], F32)]
    return pl.pallas_call(
        functools.partial(_gdn_kernel, lay=lay, reverse=reverse, direction=direction),
        name="gdn_scan", grid=(lay.tiles(GDN_TILE_ROWS),),
        in_specs=[qkv_spec, gate_spec, pl.BlockSpec(sblock, start)],
        out_specs=out_specs, out_shape=out_shape,
        scratch_shapes=[pltpu.VMEM(s0.shape[1:], F32)],
        compiler_params=_params(),
    )(qkv, gates, s0)


def _group_states(lat_state, flatten=True):
    db = lat_state.shape[0]
    tail = (-1,) if flatten else lat_state.shape[1:]
    return lat_state.reshape((db // SEQ_GROUP, SEQ_GROUP) + tail).astype(F32)


def kernel(x_prompt, x_sample, state_s5_re, state_s5_im, state_delta, state_lru, c, c_ctx, w_ada, b_ada, norm_mix_pre, norm_mix_post, norm_mlp_pre, norm_mlp_post, w_mlp_in, w_mlp_out, w_in_even, w_out_even, s5_lam_re, s5_lam_im, s5_log_dt, s5_b_re, s5_b_im, s5_c_re, s5_c_im, s5_d, gdn_conv_w, gdn_conv_b, gdn_a_log, gdn_dt_bias, gdn_o_norm, w_in_odd, w_out_odd, lru_conv_w, lru_conv_b, lru_w_r, lru_b_r, lru_w_i, lru_b_i, lru_lam):
    bp, tp, d = x_prompt.shape
    bl, tl, _ = x_sample.shape
    depth = w_ada.shape[0]
    n_dir = 2
    da = s5_d.shape[1]
    db = gdn_conv_w.shape[2] // 3
    heads = GDN_HEADS
    ngrp = s5_lam_re.shape[2]

    lay = Layout(bp // SEQ_GROUP, tp * SEQ_GROUP, bl // SEQ_GROUP, tl * SEQ_GROUP)

    cond = jnp.concatenate([c_ctx[None].astype(F32), c.astype(F32)], axis=0)
    pad = (-cond.shape[0]) % SUBLANES
    cond = jnp.pad(cond, ((0, pad), (0, 0)))
    ada = _ada_vectors(cond, w_ada, b_ada)
    mod_ctx = jnp.broadcast_to(ada[:, 0:1], (depth, SEQ_GROUP, ada.shape[-1]))[:, None]
    mod_lat = ada[:, 1:1 + bl].reshape(depth, bl // SEQ_GROUP, SEQ_GROUP, -1)
    mod = jnp.concatenate([mod_ctx, mod_lat], axis=1)

    x = _embed(lay, x_prompt.astype(F32), x_sample.astype(F32), _grid_sincos_table(tl, d))

    a_re, a_im, bb_re, bb_im = _s5_discretise(s5_lam_re, s5_lam_im, s5_log_dt, s5_b_re, s5_b_im)
    c_re_t = s5_c_re
    c_im_t = s5_c_im

    row = lambda v: v.reshape(1, -1).astype(F32)
    w_mlp_in_bf = w_mlp_in.astype(BF16)
    w_mlp_out_bf = w_mlp_out.astype(BF16)
    new_re, new_im, new_delta, new_lru = [], [], [], []
    for l in range(depth):
        mod_l = mod[l]
        if l % 2 == 0:
            e = l // 2
            n_in = w_in_even.shape[2]
            n_pad = (-n_in) % LANES
            w_in = jnp.pad(w_in_even[e], ((0, 0), (0, n_pad))).astype(BF16)
            splits = ((0, da, "plain"), (da, 2 * da, "plain"), (2 * da, 2 * da + 3 * db, "qkv"),
                      (2 * da + 3 * db, 2 * da + 4 * db, "plain"),
                      (2 * da + 4 * db, n_in + n_pad, "gates"))
            pad8 = lambda v: jnp.pad(v.reshape(1, -1).astype(F32), ((0, 0), (0, LANES - v.size)))
            n_slabs = 3 * db // LANES
            conv_w = gdn_conv_w[e].astype(F32).reshape(CONV_K, n_slabs, LANES).transpose(1, 0, 2)
            conv_b = gdn_conv_b[e].astype(F32).reshape(n_slabs, 1, LANES)
            u, za, qkv, zb, gates = _pre_mixer(
                lay, x, mod_l, row(norm_mix_pre[l]), w_in, splits,
                (conv_w, conv_b, pad8(gdn_a_log[e]), pad8(gdn_dt_bias[e])))

            cw = jnp.stack([_block_diag_out(c_re_t[e]), _block_diag_out(c_im_t[e])]).astype(BF16)
            ys, fr, fi = [], [], []
            for dd in range(n_dir):
                a8 = jnp.stack([a_re[e, dd].reshape(-1), a_im[e, dd].reshape(-1)])
                a8 = jnp.broadcast_to(a8[:, None, :], (2, SEQ_GROUP, a8.shape[-1]))
                bw = jnp.concatenate([_block_diag_in(bb_re[e, dd]), _block_diag_in(bb_im[e, dd])],
                                     axis=-1).astype(BF16)
                y, f_re, f_im = _s5_scan(lay, u, _group_states(state_s5_re[:, e, dd]),
                                         _group_states(state_s5_im[:, e, dd]),
                                         a8, bw, cw, reverse=(dd == 1))
                ys.append(y)
                fr.append(f_re.reshape(bp, ngrp, -1))
                fi.append(f_im.reshape(bp, ngrp, -1))
            new_re.append(jnp.stack(fr, axis=1))
            new_im.append(jnp.stack(fi, axis=1))

            os_, fd = [], []
            for dd in range(n_dir):
                s0 = _group_states(state_delta[:, e, dd], flatten=False)
                o_d, s_fin = _gdn_scan(lay, qkv, gates, s0, reverse=(dd == 1), direction=dd)
                os_.append(o_d)
                fd.append(s_fin.reshape((bp,) + s_fin.shape[2:]))
            new_delta.append(jnp.stack(fd, axis=1))

            acts = [u, za, ys[0], ys[1], os_[0], os_[1], zb]
            small = [row(s5_d[e]), row(gdn_o_norm[e]), row(norm_mix_post[l])]
            body, name, w_out = _tail_even_kernel, "tail_even", w_out_even[e]
        else:
            o = l // 2
            dr = lru_lam.shape[2]
            w_in = w_in_odd[o].astype(BF16)
            xb, yg = _pre_mixer(lay, x, mod_l, row(norm_mix_pre[l]), w_in,
                                ((0, dr, "conv"), (dr, 2 * dr, "plain")),
                                (lru_conv_w[o].astype(F32), row(lru_conv_b[o])))
            hs, fl = [], []
            for dd in range(n_dir):
                h, h_fin = _lru_scan(lay, xb, _group_states(state_lru[:, o, dd]),
                                     lru_w_r[o, dd].astype(BF16), lru_w_i[o, dd].astype(BF16),
                                     row(lru_b_r[o, dd]), row(lru_b_i[o, dd]), row(lru_lam[o, dd]),
                                     reverse=(dd == 1))
                hs.append(h)
                fl.append(h_fin.reshape(bp, dr))
            new_lru.append(jnp.stack(fl, axis=1))
            acts = [hs[0], hs[1], yg]
            small = [row(norm_mix_post[l])]
            body, name, w_out = _tail_odd_kernel, "tail_odd", w_out_odd[o]

        x = _layer_tail(lay, body, name, acts, x, mod_l, small, w_out.astype(BF16),
                        row(norm_mlp_pre[l]), row(norm_mlp_post[l]), w_mlp_in_bf, w_mlp_out_bf, l)

    y_prompt, y_sample = _unembed(lay, x, x_prompt.shape, x_sample.shape)
    y_prompt = y_prompt.astype(x_prompt.dtype)
    y_sample = y_sample.astype(x_sample.dtype)
    p_a = s5_lam_re.shape[3]
    new_s5_re = jnp.stack(new_re, axis=1).reshape(bp, -1, n_dir, ngrp, p_a)
    new_s5_im = jnp.stack(new_im, axis=1).reshape(bp, -1, n_dir, ngrp, p_a)
    return (y_prompt, y_sample, new_s5_re, new_s5_im,
            jnp.stack(new_delta, axis=1), jnp.stack(new_lru, axis=1))
```

```python
import functools
from typing import NamedTuple

import jax
import jax.numpy as jnp
from jax import lax
from jax.experimental import pallas as pl
from jax.experimental.pallas import tpu as pltpu

F32 = jnp.float32
BF16 = jnp.bfloat16
HIGHEST = lax.Precision.HIGHEST

EPS = 1e-6
POS_BASE = 10000.0
GRID_W = 64
N_MOD = 6
S5_GROUP = 16
S5_STATES = 64
GDN_HEADS = 4
GDN_CHUNK = 64
CONV_K = 4
CONV_LEFT = (CONV_K - 1) // 2
LRU_BLOCKS = 4
LRU_C = 8.0

SUBLANES = 8
LANES = 128
SEQ_GROUP = SUBLANES
SCAN_STEPS = 128
SCAN_ROWS = SCAN_STEPS * SEQ_GROUP
GDN_TILE_CHUNKS = 4
GDN_TILE_ROWS = GDN_TILE_CHUNKS * GDN_CHUNK * SEQ_GROUP
GDN_SEQ_BATCH = 8
ADA_COLS = 1536
RELAYOUT_STEPS = 128
TOKEN_ROWS = 512
PRE_ROWS = 1024
MLP_HIDDEN_BLOCK = 2048
TAIL_ROW_BLOCK = 128
S5_SLICE = 128
VMEM_LIMIT_BYTES = 56 * 1024 * 1024


class Layout(NamedTuple):
    n_ctx: int
    len_ctx: int
    n_lat: int
    len_lat: int

    @property
    def ctx_rows(self):
        return self.n_ctx * self.len_ctx

    @property
    def rows(self):
        return self.ctx_rows + self.n_lat * self.len_lat

    @property
    def groups(self):
        return self.n_ctx + self.n_lat

    def tiles(self, tile_rows):
        return self.ctx_rows // tile_rows + self.n_lat * (self.len_lat // tile_rows)


def _schedule(lay, tile_rows, reverse, i):
    nc = lay.len_ctx // tile_rows
    nl = lay.len_lat // tile_rows
    n_ctx_tiles = lay.n_ctx * nc
    is_ctx = i < n_ctx_tiles
    j = jnp.maximum(i - n_ctx_tiles, 0)
    ic = jnp.minimum(i, n_ctx_tiles - 1)
    g = jnp.where(is_ctx, ic // nc, lay.n_ctx + j // nl)
    step = jnp.where(is_ctx, ic % nc, j % nl)
    n = jnp.where(is_ctx, nc, nl)
    k = (n - 1 - step) if reverse else step
    base = jnp.where(is_ctx, (ic // nc) * nc, n_ctx_tiles + (j // nl) * nl)
    return base + k, g, step == 0, step == n - 1, k, n


def _params(n_axes=1):
    return pltpu.CompilerParams(dimension_semantics=("arbitrary",) * n_axes,
                                vmem_limit_bytes=VMEM_LIMIT_BYTES)


def _dot(a, b):
    return jnp.dot(a, b, preferred_element_type=F32)


def _dot_hi(a, b):
    return jnp.dot(a, b, precision=HIGHEST, preferred_element_type=F32)


def _dot_nt(a, b):
    return lax.dot_general(a, b, (((1,), (1,)), ((), ())), preferred_element_type=F32)


def _dot_tn(a, b):
    return lax.dot_general(a, b, (((0,), (0,)), ((), ())), preferred_element_type=F32)


def _rms(x):
    return x * lax.rsqrt(jnp.mean(x * x, axis=-1, keepdims=True) + EPS)


def _per_group(rows_val, vec8):
    r, c = rows_val.shape
    return rows_val.reshape(r // SEQ_GROUP, SEQ_GROUP, c), vec8[None]


def _modulate(h, shift8, scale8):
    h3, sc = _per_group(h, scale8)
    out = h3 * (1.0 + sc) + shift8[None]
    return out.reshape(h.shape)


def _gated_residual(x, y, gate8):
    y3, g = _per_group(y, gate8)
    return x + (y3 * g).reshape(x.shape)


def _mod_chunk(mod_ref, idx, d):
    return mod_ref[0, :, idx * d:(idx + 1) * d]


def _softplus(x):
    return jnp.maximum(x, 0.0) + jnp.log1p(jnp.exp(-jnp.abs(x)))


def _sigmoid(x):
    return 0.5 * jnp.tanh(0.5 * x) + 0.5


def _ada_kernel(c_ref, w_ref, b_ref, o_ref):
    s = jax.nn.silu(c_ref[...]).astype(BF16)
    o_ref[0] = _dot(s, w_ref[0].astype(BF16)) + b_ref[0]


def _ada_vectors(cond, w_ada, b_ada):
    depth, d, n = w_ada.shape
    rows = cond.shape[0]
    tn = ADA_COLS
    return pl.pallas_call(
        _ada_kernel, name="ada_vectors",
        grid=(depth, n // tn),
        in_specs=[pl.BlockSpec((rows, d), lambda l, j: (0, 0)),
                  pl.BlockSpec((1, d, tn), lambda l, j: (l, 0, j)),
                  pl.BlockSpec((1, 1, tn), lambda l, j: (l, 0, j))],
        out_specs=pl.BlockSpec((1, rows, tn), lambda l, j: (l, 0, j)),
        out_shape=jax.ShapeDtypeStruct((depth, rows, n), F32),
        compiler_params=_params(2),
    )(cond, w_ada, b_ada.reshape(depth, 1, n))


def _family_maps(lay):
    sched = functools.partial(_schedule, lay, RELAYOUT_STEPS * SEQ_GROUP, False)
    nc = lay.len_ctx // (RELAYOUT_STEPS * SEQ_GROUP)

    def ctx(i):
        _, g, _, _, k, _ = sched(i)
        on = g < lay.n_ctx
        return (jnp.minimum(g, lay.n_ctx - 1), jnp.where(on, k, nc - 1), 0)

    def lat(i):
        _, g, _, _, k, _ = sched(i)
        on = g >= lay.n_ctx
        return (jnp.maximum(g - lay.n_ctx, 0), jnp.where(on, k, 0), 0)

    def lat_steps(i):
        return lat(i)[1:]

    return ctx, lat, lat_steps


def _embed_kernel(xc_ref, xl_ref, pos_ref, o_ref, slab_ref, *, lay):
    n_ctx_tiles = lay.ctx_rows // (RELAYOUT_STEPS * SEQ_GROUP)
    n_slabs = slab_ref.shape[0]

    def relayout(x_ref, pos):
        for b in range(SEQ_GROUP):
            xb = x_ref[b] if pos is None else x_ref[b] + pos
            for j in range(n_slabs):
                slab_ref[j, pl.ds(b, RELAYOUT_STEPS, stride=SEQ_GROUP), :] = (
                    xb[:, j * LANES:(j + 1) * LANES])
        for j in range(n_slabs):
            o_ref[:, j * LANES:(j + 1) * LANES] = slab_ref[j]

    @pl.when(pl.program_id(0) < n_ctx_tiles)
    def _():
        relayout(xc_ref, None)

    @pl.when(pl.program_id(0) >= n_ctx_tiles)
    def _():
        relayout(xl_ref, pos_ref[...])


def _embed(lay, x_ctx, x_lat, table):
    d = x_ctx.shape[-1]
    rows_tile = RELAYOUT_STEPS * SEQ_GROUP
    ctx, lat, lat_steps = _family_maps(lay)
    block = (SEQ_GROUP, RELAYOUT_STEPS, d)
    return pl.pallas_call(
        functools.partial(_embed_kernel, lay=lay), name="embed",
        grid=(lay.tiles(rows_tile),),
        in_specs=[pl.BlockSpec(block, ctx), pl.BlockSpec(block, lat),
                  pl.BlockSpec((RELAYOUT_STEPS, d), lat_steps)],
        out_specs=pl.BlockSpec((rows_tile, d), lambda i: (i, 0)),
        out_shape=jax.ShapeDtypeStruct((lay.rows, d), F32),
        scratch_shapes=[pltpu.VMEM((d // LANES, rows_tile, LANES), F32)],
        compiler_params=_params(),
    )(x_ctx, x_lat, table)


def _unembed_kernel(x_ref, yc_ref, yl_ref, slab_ref, *, lay):
    n_ctx_tiles = lay.ctx_rows // (RELAYOUT_STEPS * SEQ_GROUP)
    n_slabs = slab_ref.shape[0]
    for j in range(n_slabs):
        slab_ref[j] = x_ref[:, j * LANES:(j + 1) * LANES]

    def relayout(y_ref):
        for b in range(SEQ_GROUP):
            for j in range(n_slabs):
                y_ref[b, :, j * LANES:(j + 1) * LANES] = (
                    slab_ref[j, pl.ds(b, RELAYOUT_STEPS, stride=SEQ_GROUP), :])

    @pl.when(pl.program_id(0) < n_ctx_tiles)
    def _():
        relayout(yc_ref)

    @pl.when(pl.program_id(0) >= n_ctx_tiles)
    def _():
        relayout(yl_ref)


def _unembed(lay, x, shape_ctx, shape_lat):
    d = x.shape[-1]
    rows_tile = RELAYOUT_STEPS * SEQ_GROUP
    ctx, lat, _ = _family_maps(lay)
    block = (SEQ_GROUP, RELAYOUT_STEPS, d)
    return pl.pallas_call(
        functools.partial(_unembed_kernel, lay=lay), name="unembed",
        grid=(lay.tiles(rows_tile),),
        in_specs=[pl.BlockSpec((rows_tile, d), lambda i: (i, 0))],
        out_specs=[pl.BlockSpec(block, ctx), pl.BlockSpec(block, lat)],
        out_shape=[jax.ShapeDtypeStruct(shape_ctx, F32), jax.ShapeDtypeStruct(shape_lat, F32)],
        scratch_shapes=[pltpu.VMEM((d // LANES, rows_tile, LANES), F32)],
        compiler_params=_params(),
    )(x)


def _grid_sincos_table(n_tokens, d_model):
    rows = n_tokens // GRID_W
    row = jnp.repeat(jnp.arange(rows, dtype=F32), GRID_W)
    col = jnp.tile(jnp.arange(GRID_W, dtype=F32), rows)
    n_freq = d_model // 4
    omega = POS_BASE ** (-jnp.arange(n_freq, dtype=F32) / n_freq)
    ar = row[:, None] * omega
    ac = col[:, None] * omega
    return jnp.concatenate([jnp.sin(ar), jnp.cos(ar), jnp.sin(ac), jnp.cos(ac)], axis=-1)


def _mod_index(lay, tile_rows):
    def index(i):
        row = i * tile_rows
        lat = jnp.maximum(row - lay.ctx_rows, 0) // lay.len_lat
        return jnp.where(row < lay.ctx_rows, 0, 1 + lat)
    return index


def _pre_kernel(x_ref, xprev_ref, xnext_ref, mod_ref, gain_ref, w_ref, *refs, splits, lay):
    n_par = 2 * sum(kind != "plain" for _, _, kind in splits)
    par_refs, out_refs = refs[:n_par], refs[n_par:]
    _, _, _, _, kpos, npos = _schedule(lay, PRE_ROWS, False, pl.program_id(0))
    r, d = x_ref.shape
    lead = xnext_ref.shape[0]
    xall = jnp.concatenate([xnext_ref[...], x_ref[...], xprev_ref[...],
                            jnp.zeros_like(xprev_ref[...])], axis=0)
    h = _rms(xall) * gain_ref[...]
    h = _modulate(h, _mod_chunk(mod_ref, 0, d), _mod_chunk(mod_ref, 1, d)).astype(BF16)
    hmain = h[lead:lead + r]
    p = 0
    for (a, b, kind), o_ref in zip(splits, out_refs):
        if kind == "plain":
            o_ref[...] = _dot(hmain, w_ref[:, a:b])
            continue
        p0_ref, p1_ref = par_refs[p:p + 2]
        p += 2
        if kind == "gates":
            ab = _dot(hmain, w_ref[:, a:b])
            g_all = -jnp.exp(p0_ref[...]) * _softplus(ab + p1_ref[...])
            lane_id = lax.broadcasted_iota(jnp.int32, ab.shape, 1)
            o_ref[...] = jnp.where(lane_id < 2 * GDN_HEADS, g_all, _sigmoid(ab))
            continue
        acc = _dot(h, w_ref[:, a:b])
        nxt, main, prev = acc[0:lead], acc[lead:lead + r], acc[lead + r:lead + r + SEQ_GROUP]
        if kind == "conv":
            o_ref[...] = _conv_centred(main, prev, nxt, kpos, npos, p0_ref[...], p1_ref[...])
            continue
        dk = LANES
        for j in range((b - a) // dk):
            cols = slice(j * dk, (j + 1) * dk)
            x = _conv_centred(main[:, cols], prev[:, cols], nxt[:, cols], kpos, npos,
                              p0_ref[j], p1_ref[j])
            x = x * _sigmoid(x)
            if j < 2 * GDN_HEADS:
                x = x * lax.rsqrt(jnp.sum(x * x, axis=-1, keepdims=True) + EPS)
            if j < GDN_HEADS:
                x = x * (dk ** -0.5)
            o_ref[j] = x


def _pre_mixer(lay, x, mod, gain, w, splits, params):
    rows, d = x.shape
    n = w.shape[1]
    midx = _mod_index(lay, PRE_ROWS)
    prev_spec, next_spec = _halo_specs(lay, PRE_ROWS // SEQ_GROUP, d)
    out_specs, out_shapes = [], []
    for a, b, kind in splits:
        if kind == "qkv":
            ns = (b - a) // LANES
            out_specs.append(pl.BlockSpec((ns, PRE_ROWS, LANES), lambda i: (0, i, 0)))
            out_shapes.append(jax.ShapeDtypeStruct((ns, rows, LANES), F32))
        else:
            out_specs.append(pl.BlockSpec((PRE_ROWS, b - a), lambda i: (i, 0)))
            out_shapes.append(jax.ShapeDtypeStruct((rows, b - a), F32))
    const = lambda a: pl.BlockSpec(a.shape, lambda i: (0,) * a.ndim)
    return pl.pallas_call(
        functools.partial(_pre_kernel, splits=splits, lay=lay), name="pre_mixer",
        grid=(rows // PRE_ROWS,),
        in_specs=[pl.BlockSpec((PRE_ROWS, d), lambda i: (i, 0)), prev_spec, next_spec,
                  pl.BlockSpec((1, SEQ_GROUP, N_MOD * d), lambda i: (midx(i), 0, 0)),
                  pl.BlockSpec((1, d), lambda i: (0, 0)),
                  pl.BlockSpec((d, n), lambda i: (0, 0), pipeline_mode=pl.Buffered(1))]
                 + [const(a) for a in params],
        out_specs=out_specs, out_shape=out_shapes,
        compiler_params=_params(),
    )(x, x, x, mod, gain, w, *params)


def _finish_sublayer(x, out, gain, gate8):
    return _gated_residual(x, _rms(out) * gain, gate8)


def _mlp_sublayer(x, mod_ref, gpre_ref, gpost_ref, w1_ref, w2_ref):
    d = x.shape[-1]
    dff = w1_ref.shape[-1]
    h = _rms(x) * gpre_ref[...]
    h = _modulate(h, _mod_chunk(mod_ref, 3, d), _mod_chunk(mod_ref, 4, d)).astype(BF16)
    out = None
    for a in range(0, dff, MLP_HIDDEN_BLOCK):
        f = jnp.square(jnp.maximum(_dot(h, w1_ref[0, :, a:a + MLP_HIDDEN_BLOCK]), 0.0))
        part = _dot(f.astype(BF16), w2_ref[0, a:a + MLP_HIDDEN_BLOCK, :])
        out = part if out is None else out + part
    return _finish_sublayer(x, out, gpost_ref[...], _mod_chunk(mod_ref, 5, d))


def _tail_odd_kernel(hf_ref, hb_ref, yg_ref, x_ref, mod_ref, gmix_ref, wout_ref,
                     gpre_ref, gpost_ref, w1_ref, w2_ref, o_ref):
    d = x_ref.shape[-1]
    xs = []
    for r0 in range(0, x_ref.shape[0], TAIL_ROW_BLOCK):
        rs = slice(r0, r0 + TAIL_ROW_BLOCK)
        y = (hf_ref[rs, :] + hb_ref[rs, :]) * jax.nn.gelu(yg_ref[rs, :])
        out = _dot(y.astype(BF16), wout_ref[...])
        xs.append(_finish_sublayer(x_ref[rs, :], out, gmix_ref[...], _mod_chunk(mod_ref, 2, d)))
    x = jnp.concatenate(xs, axis=0)
    o_ref[...] = _mlp_sublayer(x, mod_ref, gpre_ref, gpost_ref, w1_ref, w2_ref)


def _tail_even_kernel(u_ref, za_ref, yf_ref, yb_ref, of_ref, ob_ref, zb_ref, x_ref, mod_ref,
                      dskip_ref, onorm_ref, gmix_ref, wout_ref,
                      gpre_ref, gpost_ref, w1_ref, w2_ref, o_ref):
    d = x_ref.shape[-1]
    da = u_ref.shape[-1]
    heads, _, dv = of_ref.shape
    xs = []
    for r0 in range(0, x_ref.shape[0], TAIL_ROW_BLOCK):
        rs = slice(r0, r0 + TAIL_ROW_BLOCK)
        ya = jax.nn.gelu(yf_ref[rs, :] + yb_ref[rs, :] + u_ref[rs, :] * dskip_ref[...])
        ya = ya * _sigmoid(za_ref[rs, :])
        out = _dot(ya.astype(BF16), wout_ref[0:da, :])
        for h in range(heads):
            o = of_ref[h, rs, :] + ob_ref[h, rs, :]
            zb = zb_ref[rs, h * dv:(h + 1) * dv]
            oh = _rms(o) * onorm_ref[...] * (zb * _sigmoid(zb))
            out = out + _dot(oh.astype(BF16), wout_ref[da + h * dv:da + (h + 1) * dv, :])
        xs.append(_finish_sublayer(x_ref[rs, :], out, gmix_ref[...], _mod_chunk(mod_ref, 2, d)))
    x = jnp.concatenate(xs, axis=0)
    o_ref[...] = _mlp_sublayer(x, mod_ref, gpre_ref, gpost_ref, w1_ref, w2_ref)


def _layer_tail(lay, body, name, acts, x, mod, small, wout, gpre, gpost, w1, w2, layer):
    rows, d = x.shape
    midx = _mod_index(lay, TOKEN_ROWS)

    def act_spec(a):
        if a.ndim == 3:
            return pl.BlockSpec((a.shape[0], TOKEN_ROWS, a.shape[2]), lambda i: (0, i, 0))
        return pl.BlockSpec((TOKEN_ROWS, a.shape[1]), lambda i: (i, 0))

    const = lambda a: pl.BlockSpec(a.shape, lambda i: (0, 0))
    resident = lambda a: pl.BlockSpec(a.shape, lambda i: (0, 0), pipeline_mode=pl.Buffered(1))
    of_layer = lambda a: pl.BlockSpec((1,) + a.shape[1:], lambda i: (layer, 0, 0),
                                      pipeline_mode=pl.Buffered(1))
    return pl.pallas_call(
        body, name=name,
        grid=(rows // TOKEN_ROWS,),
        in_specs=[act_spec(a) for a in acts] + [
            act_spec(x), pl.BlockSpec((1, SEQ_GROUP, N_MOD * d), lambda i: (midx(i), 0, 0))]
            + [const(s) for s in small] + [resident(wout), const(gpre), const(gpost),
                                           of_layer(w1), of_layer(w2)],
        out_specs=act_spec(x),
        out_shape=jax.ShapeDtypeStruct((rows, d), F32),
        compiler_params=_params(),
    )(*acts, x, mod, *small, wout, gpre, gpost, w1, w2)


def _s5_discretise_kernel(lr_ref, li_ref, ldt_ref, bre_ref, bim_ref,
                          ar_ref, ai_ref, bbre_ref, bbim_ref):
    lr, li = lr_ref[...], li_ref[...]
    dt = jnp.exp(ldt_ref[...])
    mag = jnp.exp(lr * dt)
    ar = mag * jnp.cos(li * dt)
    ai = mag * jnp.sin(li * dt)
    den = lr * lr + li * li
    fr = ((ar - 1.0) * lr + ai * li) / den
    fi = (ai * lr - (ar - 1.0) * li) / den
    ar_ref[...] = ar
    ai_ref[...] = ai
    bbre_ref[...] = fr * bre_ref[...] - fi * bim_ref[...]
    bbim_ref[...] = fr * bim_ref[...] + fi * bre_ref[...]


def _s5_discretise(lam_re, lam_im, log_dt, b_re, b_im):
    e, nd, g, p = lam_re.shape
    c = b_re.shape[-1]
    full = (e, nd, g, c, p)
    flat = lambda a: jnp.broadcast_to(a, full).reshape(-1, p)
    lr = flat(lam_re[:, :, :, None, :])
    li = flat(lam_im[:, :, :, None, :])
    ldt = flat(log_dt[:, :, :, None, None])
    bre = flat(b_re.transpose(0, 1, 3, 2)[:, None])
    bim = flat(b_im.transpose(0, 1, 3, 2)[:, None])
    shape = jax.ShapeDtypeStruct(lr.shape, F32)
    ar, ai, bbre, bbim = pl.pallas_call(
        _s5_discretise_kernel, name="s5_discretise", out_shape=[shape] * 4,
        compiler_params=pltpu.CompilerParams(vmem_limit_bytes=VMEM_LIMIT_BYTES),
    )(lr, li, ldt, bre, bim)
    ar = ar.reshape(full)[:, :, :, 0, :]
    ai = ai.reshape(full)[:, :, :, 0, :]
    return ar, ai, bbre.reshape(full), bbim.reshape(full)


def _block_diag_in(bb):
    g, c, p = bb.shape
    per = S5_SLICE // c
    n = g // per
    blocks = bb.reshape(n, per, c, p)
    eye = jnp.eye(per, dtype=bb.dtype)
    return jnp.einsum('nicp,ij->nicjp', blocks, eye).reshape(n, per * c, per * p)


def _block_diag_out(cc):
    g, c, p = cc.shape
    per = S5_SLICE // c
    n = g // per
    blocks = cc.reshape(n, per, c, p)
    eye = jnp.eye(per, dtype=cc.dtype)
    return jnp.einsum('nicp,ij->nipjc', blocks, eye).reshape(n, per * p, per * c)


def _s5_scan_kernel(u_ref, h0re_ref, h0im_ref, a_ref, bw_ref, cw_ref,
                    y_ref, fre_ref, fim_ref, hre_ref, him_ref, bre_ref, bim_ref,
                    *, lay, reverse):
    _, grp, first, last, _, _ = _schedule(lay, SCAN_ROWS, reverse, pl.program_id(0))
    is_ctx = grp < lay.n_ctx
    n_slices = bw_ref.shape[0]
    wide = bw_ref.shape[2] // 2

    @pl.when(first)
    def _():
        hre_ref[...] = jnp.where(is_ctx, 0.0, h0re_ref[0])
        him_ref[...] = jnp.where(is_ctx, 0.0, h0im_ref[0])

    for j in range(n_slices):
        uj = u_ref[:, j * S5_SLICE:(j + 1) * S5_SLICE].astype(BF16)
        bb = _dot(uj, bw_ref[j])
        bre_ref[:, j * wide:(j + 1) * wide] = bb[:, :wide]
        bim_ref[:, j * wide:(j + 1) * wide] = bb[:, wide:]

    steps = range(SCAN_STEPS - 1, -1, -1) if reverse else range(SCAN_STEPS)
    for j in range(n_slices):
        cols = slice(j * wide, (j + 1) * wide)
        ar = a_ref[0, :, cols]
        ai = a_ref[1, :, cols]
        hr = hre_ref[:, cols]
        hi = him_ref[:, cols]
        for t in steps:
            rows = slice(t * SEQ_GROUP, (t + 1) * SEQ_GROUP)
            nr = ar * hr - ai * hi + bre_ref[rows, cols]
            ni = ar * hi + ai * hr + bim_ref[rows, cols]
            bre_ref[rows, cols] = nr
            bim_ref[rows, cols] = ni
            hr, hi = nr, ni
        hre_ref[:, cols] = hr
        him_ref[:, cols] = hi

    for j in range(n_slices):
        cols = slice(j * wide, (j + 1) * wide)
        yj = _dot(bre_ref[:, cols].astype(BF16), cw_ref[0, j])
        yj = yj - _dot(bim_ref[:, cols].astype(BF16), cw_ref[1, j])
        y_ref[:, j * S5_SLICE:(j + 1) * S5_SLICE] = yj

    @pl.when(last & is_ctx)
    def _():
        fre_ref[0] = hre_ref[...]
        fim_ref[0] = him_ref[...]


def _state_maps(lay, sched, ndim):
    pad = (0,) * (ndim - 1)
    start = lambda i: (jnp.maximum(sched(i)[1] - lay.n_ctx, 0),) + pad
    final = lambda i: (jnp.minimum(sched(i)[1], lay.n_ctx - 1),) + pad
    return start, final


def _s5_scan(lay, u, h0re, h0im, a8, bw, cw, reverse):
    rows, da = u.shape
    ns = h0re.shape[-1]
    sched = functools.partial(_schedule, lay, SCAN_ROWS, reverse)
    tile = lambda i: (sched(i)[0], 0)
    start, final = _state_maps(lay, sched, 3)
    state_shape = jax.ShapeDtypeStruct((lay.n_ctx, SEQ_GROUP, ns), F32)
    return pl.pallas_call(
        functools.partial(_s5_scan_kernel, lay=lay, reverse=reverse), name="s5_scan",
        grid=(lay.tiles(SCAN_ROWS),),
        in_specs=[pl.BlockSpec((SCAN_ROWS, da), tile),
                  pl.BlockSpec((1, SEQ_GROUP, ns), start),
                  pl.BlockSpec((1, SEQ_GROUP, ns), start),
                  pl.BlockSpec(a8.shape, lambda i: (0, 0, 0)),
                  pl.BlockSpec(bw.shape, lambda i: (0, 0, 0)),
                  pl.BlockSpec(cw.shape, lambda i: (0, 0, 0, 0))],
        out_specs=[pl.BlockSpec((SCAN_ROWS, da), tile),
                   pl.BlockSpec((1, SEQ_GROUP, ns), final),
                   pl.BlockSpec((1, SEQ_GROUP, ns), final)],
        out_shape=[jax.ShapeDtypeStruct((rows, da), F32), state_shape, state_shape],
        scratch_shapes=[pltpu.VMEM((SEQ_GROUP, ns), F32), pltpu.VMEM((SEQ_GROUP, ns), F32),
                        pltpu.VMEM((SCAN_ROWS, ns), F32), pltpu.VMEM((SCAN_ROWS, ns), F32)],
        compiler_params=_params(),
    )(u, h0re, h0im, a8, bw, cw)


def _halo_specs(lay, steps, channels, reverse=False, slabs=None):
    sched = functools.partial(_schedule, lay, steps * SEQ_GROUP, reverse)
    last_pair = lay.rows // (2 * SEQ_GROUP) - 1
    lead = () if slabs is None else (0,)
    shape = () if slabs is None else (slabs,)

    def prev(i):
        t = sched(i)[0]
        return lead + (jnp.maximum(t * steps - 1, 0), 0)

    def nxt(i):
        t = sched(i)[0]
        return lead + (jnp.minimum((t + 1) * (steps // 2), last_pair), 0)

    return (pl.BlockSpec(shape + (SEQ_GROUP, channels), prev),
            pl.BlockSpec(shape + (2 * SEQ_GROUP, channels), nxt))


def _conv_centred(x, prev, nxt, k, n, w, b):
    prev = jnp.where(k > 0, prev, 0.0)
    nxt = jnp.where(k < n - 1, nxt, 0.0)
    xp = jnp.concatenate([prev, x, nxt], axis=0)
    r = x.shape[0]
    out = b + xp[0:r] * w[0:1, :]
    for j in range(1, CONV_K):
        out = out + xp[j * SEQ_GROUP:j * SEQ_GROUP + r] * w[j:j + 1, :]
    return out


def _lru_scan_kernel(x_ref, h0_ref, wr_ref, wi_ref, br_ref, bi_ref, lam_ref,
                     h_ref, fin_ref, st_ref, a_ref, b_ref, *, lay, reverse):
    _, grp, first, last, _, _ = _schedule(lay, SCAN_ROWS, reverse, pl.program_id(0))
    is_ctx = grp < lay.n_ctx

    @pl.when(first)
    def _():
        st_ref[...] = jnp.where(is_ctx, 0.0, h0_ref[0])

    x = x_ref[...]
    xb = x.astype(BF16)
    bs = wr_ref.shape[-1]
    neg_sp = -LRU_C * _softplus(-lam_ref[...])
    for blk in range(LRU_BLOCKS):
        cols = slice(blk * bs, (blk + 1) * bs)
        r = _sigmoid(_dot(xb[:, cols], wr_ref[blk]) + br_ref[:, cols])
        gi = _sigmoid(_dot(xb[:, cols], wi_ref[blk]) + bi_ref[:, cols])
        log_a = neg_sp[:, cols] * r
        th = jnp.tanh(log_a)
        a_ref[:, cols] = jnp.exp(log_a)
        y = -2.0 * th / (1.0 - th)
        root = jnp.where(y > 0.0, y * lax.rsqrt(y), 0.0)
        b_ref[:, cols] = root * (gi * x[:, cols])

    steps = range(SCAN_STEPS - 1, -1, -1) if reverse else range(SCAN_STEPS)
    h = st_ref[...]
    for t in steps:
        rows = slice(t * SEQ_GROUP, (t + 1) * SEQ_GROUP)
        h = a_ref[rows, :] * h + b_ref[rows, :]
        h_ref[rows, :] = h
    st_ref[...] = h

    @pl.when(last & is_ctx)
    def _():
        fin_ref[0] = st_ref[...]


def _lru_scan(lay, xb, h0, wr, wi, br, bi, lam, reverse):
    rows, d = xb.shape
    sched = functools.partial(_schedule, lay, SCAN_ROWS, reverse)
    tile = lambda i: (sched(i)[0], 0)
    start, final = _state_maps(lay, sched, 3)
    const2 = lambda a: pl.BlockSpec(a.shape, lambda i: (0, 0))
    const3 = lambda a: pl.BlockSpec(a.shape, lambda i: (0, 0, 0))
    return pl.pallas_call(
        functools.partial(_lru_scan_kernel, lay=lay, reverse=reverse), name="lru_scan",
        grid=(lay.tiles(SCAN_ROWS),),
        in_specs=[pl.BlockSpec((SCAN_ROWS, d), tile),
                  pl.BlockSpec((1, SEQ_GROUP, d), start),
                  const3(wr), const3(wi), const2(br), const2(bi), const2(lam)],
        out_specs=[pl.BlockSpec((SCAN_ROWS, d), tile), pl.BlockSpec((1, SEQ_GROUP, d), final)],
        out_shape=[jax.ShapeDtypeStruct((rows, d), F32),
                   jax.ShapeDtypeStruct((lay.n_ctx, SEQ_GROUP, d), F32)],
        scratch_shapes=[pltpu.VMEM((SEQ_GROUP, d), F32),
                        pltpu.VMEM((SCAN_ROWS, d), F32), pltpu.VMEM((SCAN_ROWS, d), F32)],
        compiler_params=_params(),
    )(xb, h0, wr, wi, br, bi, lam)


def _unit_triangular_inverses(lmats, eye, blocks):
    c = lmats[0].shape[0]
    zero = jnp.zeros((), lmats[0].dtype)
    tbs = [jnp.where(blocks[0], -l, zero) for l in lmats]
    invs = [eye + tb.astype(F32) for tb in tbs]
    ps = [_dot(tb, tb) for tb in tbs]
    ys = [_dot(jnp.concatenate([x, p], axis=0).astype(BF16), p.astype(BF16))
          for x, p in zip(invs, ps)]
    invs = [x + y[:c] for x, y in zip(invs, ys)]
    zs = [_dot(x.astype(BF16), y[c:].astype(BF16)) for x, y in zip(invs, ys)]
    invs = [x + z for x, z in zip(invs, zs)]
    inside = blocks[0]
    for outer in blocks[1:] + [None]:
        off = ~inside if outer is None else (outer & ~inside)
        es = [jnp.where(off, l, zero) for l in lmats]
        xbs = [x.astype(BF16) for x in invs]
        xes = [_dot(xb, e).astype(BF16) for xb, e in zip(xbs, es)]
        xexs = [_dot(xe, xb) for xe, xb in zip(xes, xbs)]
        invs = [x - xex for x, xex in zip(invs, xexs)]
        inside = outer
    return invs


def _gdn_kernel(qkv_ref, gate_ref, s0_ref, o_ref, fin_ref, s_ref, *, lay, reverse, direction):
    _, grp, first, last, _, _ = _schedule(lay, GDN_TILE_ROWS, reverse, pl.program_id(0))
    is_ctx = grp < lay.n_ctx
    c = GDN_CHUNK

    @pl.when(first)
    def _():
        s_ref[...] = jnp.where(is_ctx, 0.0, s0_ref[0])

    ri = lax.broadcasted_iota(jnp.int32, (c, c), 0)
    ci = lax.broadcasted_iota(jnp.int32, (c, c), 1)
    incl = (ri <= ci) if reverse else (ri >= ci)
    strict = (ri < ci) if reverse else (ri > ci)
    tri = jnp.where(incl, 1.0, 0.0).astype(F32)
    eye = jnp.where(ri == ci, 1.0, 0.0).astype(F32)
    blocks = []
    shift = 3
    while (1 << shift) < c:
        blocks.append((ri >> shift) == (ci >> shift))
        shift += 1
    end_row = 0 if reverse else c - 1
    chunks = range(GDN_TILE_CHUNKS - 1, -1, -1) if reverse else range(GDN_TILE_CHUNKS)
    for ch in chunks:
        seq_rows = lambda b, ch=ch: pl.ds(ch * c * SEQ_GROUP + b, c, stride=SEQ_GROUP)
        for b0 in range(0, SEQ_GROUP, GDN_SEQ_BATCH):
            _delta_rule_chunk(range(b0, b0 + GDN_SEQ_BATCH), qkv_ref, gate_ref, s_ref, o_ref,
                              seq_rows, (incl, strict, tri, eye, blocks), end_row, direction)

    @pl.when(last & is_ctx)
    def _():
        fin_ref[0] = s_ref[...]


def _delta_rule_chunk(seqs, qkv_ref, gate_ref, s_ref, o_ref, seq_rows, masks, end_row, direction):
    incl, strict, tri, eye, blocks = masks
    c = GDN_CHUNK
    dk = qkv_ref.shape[-1]
    keys = [(b, h) for b in seqs for h in range(GDN_HEADS)]

    lmats, qks, rhss, qes, kds, s_decays = [], {}, {}, {}, {}, {}
    for b in seqs:
        gates = gate_ref[seq_rows(b), :]
        gc = _dot_hi(tri, gates)
        gc_t = gc.T
        for h in range(GDN_HEADS):
            lane = direction * GDN_HEADS + h
            q = qkv_ref[h, seq_rows(b), :]
            k = qkv_ref[GDN_HEADS + h, seq_rows(b), :]
            v = qkv_ref[2 * GDN_HEADS + h, seq_rows(b), :]
            beta = gates[:, 2 * GDN_HEADS + lane:2 * GDN_HEADS + lane + 1]
            gcol = gc[:, lane:lane + 1]
            grow = gc_t[lane:lane + 1, :]
            egc = jnp.exp(gcol)
            g_end = gcol[end_row:end_row + 1, :]
            decay = jnp.exp(jnp.where(incl, gcol - grow, -1e30))
            kb = k * beta
            gram = _dot_nt(jnp.concatenate([kb, q], axis=0).astype(BF16), k.astype(BF16))
            lmats.append(jnp.where(strict, gram[:c] * decay, 0.0).astype(BF16))
            qks[b, h] = jnp.where(incl, gram[c:] * decay, 0.0).astype(BF16)
            rhss[b, h] = jnp.concatenate([v * beta, kb * egc], axis=1).astype(BF16)
            qes[b, h] = (q * egc).astype(BF16)
            kds[b, h] = (k * jnp.exp(g_end - gcol)).astype(BF16)
            s_decays[b, h] = jnp.exp(g_end)
    invs = _unit_triangular_inverses(lmats, eye, blocks)
    uws = {key: _dot(inv.astype(BF16), rhss[key]) for key, inv in zip(keys, invs)}

    ss = {key: s_ref[key[0], key[1]] for key in keys}
    projs = {key: _dot(jnp.concatenate([uws[key][:, dk:].astype(BF16), qes[key]], axis=0),
                       ss[key].astype(BF16)) for key in keys}
    v_news = {key: (uws[key][:, :dk] - projs[key][:c]).astype(BF16) for key in keys}
    for key in keys:
        b, h = key
        s_ref[b, h] = ss[key] * s_decays[key] + _dot_tn(kds[key], v_news[key])
    for key in keys:
        b, h = key
        o_ref[h, seq_rows(b), :] = projs[key][c:] + _dot(qks[key], v_news[key])


def _gdn_scan(lay, qkv, gates, s0, reverse, direction):
    slabs, rows, dk = qkv.shape
    sched = functools.partial(_schedule, lay, GDN_TILE_ROWS, reverse)
    tile3 = lambda i: (0, sched(i)[0], 0)
    tile2 = lambda i: (sched(i)[0], 0)
    start, final = _state_maps(lay, sched, 5)
    sblock = (1,) + s0.shape[1:]
    qkv_spec = pl.BlockSpec((slabs, GDN_TILE_ROWS, dk), tile3)
    gate_spec = pl.BlockSpec((GDN_TILE_ROWS, LANES), tile2)
    out_specs = [pl.BlockSpec((GDN_HEADS, GDN_TILE_ROWS, dk), tile3), pl.BlockSpec(sblock, final)]
    out_shape = [jax.ShapeDtypeStruct((GDN_HEADS, rows, dk), F32),
                 jax.ShapeDtypeStruct((lay.n_ctx,) + s0.shape[1:], F32)]
    return pl.pallas_call(
        functools.partial(_gdn_kernel, lay=lay, reverse=reverse, direction=direction),
        name="gdn_scan", grid=(lay.tiles(GDN_TILE_ROWS),),
        in_specs=[qkv_spec, gate_spec, pl.BlockSpec(sblock, start)],
        out_specs=out_specs, out_shape=out_shape,
        scratch_shapes=[pltpu.VMEM(s0.shape[1:], F32)],
        compiler_params=_params(),
    )(qkv, gates, s0)


def _group_states(lat_state, flatten=True):
    db = lat_state.shape[0]
    tail = (-1,) if flatten else lat_state.shape[1:]
    return lat_state.reshape((db // SEQ_GROUP, SEQ_GROUP) + tail).astype(F32)


def kernel(x_prompt, x_sample, state_s5_re, state_s5_im, state_delta, state_lru, c, c_ctx, w_ada, b_ada, norm_mix_pre, norm_mix_post, norm_mlp_pre, norm_mlp_post, w_mlp_in, w_mlp_out, w_in_even, w_out_even, s5_lam_re, s5_lam_im, s5_log_dt, s5_b_re, s5_b_im, s5_c_re, s5_c_im, s5_d, gdn_conv_w, gdn_conv_b, gdn_a_log, gdn_dt_bias, gdn_o_norm, w_in_odd, w_out_odd, lru_conv_w, lru_conv_b, lru_w_r, lru_b_r, lru_w_i, lru_b_i, lru_lam):
    bp, tp, d = x_prompt.shape
    bl, tl, _ = x_sample.shape
    depth = w_ada.shape[0]
    n_dir = 2
    da = s5_d.shape[1]
    db = gdn_conv_w.shape[2] // 3
    heads = GDN_HEADS
    ngrp = s5_lam_re.shape[2]

    lay = Layout(bp // SEQ_GROUP, tp * SEQ_GROUP, bl // SEQ_GROUP, tl * SEQ_GROUP)

    cond = jnp.concatenate([c_ctx[None].astype(F32), c.astype(F32)], axis=0)
    pad = (-cond.shape[0]) % SUBLANES
    cond = jnp.pad(cond, ((0, pad), (0, 0)))
    ada = _ada_vectors(cond, w_ada, b_ada)
    mod_ctx = jnp.broadcast_to(ada[:, 0:1], (depth, SEQ_GROUP, ada.shape[-1]))[:, None]
    mod_lat = ada[:, 1:1 + bl].reshape(depth, bl // SEQ_GROUP, SEQ_GROUP, -1)
    mod = jnp.concatenate([mod_ctx, mod_lat], axis=1)

    x = _embed(lay, x_prompt.astype(F32), x_sample.astype(F32), _grid_sincos_table(tl, d))

    a_re, a_im, bb_re, bb_im = _s5_discretise(s5_lam_re, s5_lam_im, s5_log_dt, s5_b_re, s5_b_im)
    c_re_t = s5_c_re
    c_im_t = s5_c_im

    row = lambda v: v.reshape(1, -1).astype(F32)
    w_mlp_in_bf = w_mlp_in.astype(BF16)
    w_mlp_out_bf = w_mlp_out.astype(BF16)
    new_re, new_im, new_delta, new_lru = [], [], [], []
    for l in range(depth):
        mod_l = mod[l]
        if l % 2 == 0:
            e = l // 2
            n_in = w_in_even.shape[2]
            n_pad = (-n_in) % LANES
            w_in = jnp.pad(w_in_even[e], ((0, 0), (0, n_pad))).astype(BF16)
            splits = ((0, da, "plain"), (da, 2 * da, "plain"), (2 * da, 2 * da + 3 * db, "qkv"),
                      (2 * da + 3 * db, 2 * da + 4 * db, "plain"),
                      (2 * da + 4 * db, n_in + n_pad, "gates"))
            pad8 = lambda v: jnp.pad(v.reshape(1, -1).astype(F32), ((0, 0), (0, LANES - v.size)))
            n_slabs = 3 * db // LANES
            conv_w = gdn_conv_w[e].astype(F32).reshape(CONV_K, n_slabs, LANES).transpose(1, 0, 2)
            conv_b = gdn_conv_b[e].astype(F32).reshape(n_slabs, 1, LANES)
            u, za, qkv, zb, gates = _pre_mixer(
                lay, x, mod_l, row(norm_mix_pre[l]), w_in, splits,
                (conv_w, conv_b, pad8(gdn_a_log[e]), pad8(gdn_dt_bias[e])))

            cw = jnp.stack([_block_diag_out(c_re_t[e]), _block_diag_out(c_im_t[e])]).astype(BF16)
            ys, fr, fi = [], [], []
            for dd in range(n_dir):
                a8 = jnp.stack([a_re[e, dd].reshape(-1), a_im[e, dd].reshape(-1)])
                a8 = jnp.broadcast_to(a8[:, None, :], (2, SEQ_GROUP, a8.shape[-1]))
                bw = jnp.concatenate([_block_diag_in(bb_re[e, dd]), _block_diag_in(bb_im[e, dd])],
                                     axis=-1).astype(BF16)
                y, f_re, f_im = _s5_scan(lay, u, _group_states(state_s5_re[:, e, dd]),
                                         _group_states(state_s5_im[:, e, dd]),
                                         a8, bw, cw, reverse=(dd == 1))
                ys.append(y)
                fr.append(f_re.reshape(bp, ngrp, -1))
                fi.append(f_im.reshape(bp, ngrp, -1))
            new_re.append(jnp.stack(fr, axis=1))
            new_im.append(jnp.stack(fi, axis=1))

            os_, fd = [], []
            for dd in range(n_dir):
                s0 = _group_states(state_delta[:, e, dd], flatten=False)
                o_d, s_fin = _gdn_scan(lay, qkv, gates, s0, reverse=(dd == 1), direction=dd)
                os_.append(o_d)
                fd.append(s_fin.reshape((bp,) + s_fin.shape[2:]))
            new_delta.append(jnp.stack(fd, axis=1))

            acts = [u, za, ys[0], ys[1], os_[0], os_[1], zb]
            small = [row(s5_d[e]), row(gdn_o_norm[e]), row(norm_mix_post[l])]
            body, name, w_out = _tail_even_kernel, "tail_even", w_out_even[e]
        else:
            o = l // 2
            dr = lru_lam.shape[2]
            w_in = w_in_odd[o].astype(BF16)
            xb, yg = _pre_mixer(lay, x, mod_l, row(norm_mix_pre[l]), w_in,
                                ((0, dr, "conv"), (dr, 2 * dr, "plain")),
                                (lru_conv_w[o].astype(F32), row(lru_conv_b[o])))
            hs, fl = [], []
            for dd in range(n_dir):
                h, h_fin = _lru_scan(lay, xb, _group_states(state_lru[:, o, dd]),
                                     lru_w_r[o, dd].astype(BF16), lru_w_i[o, dd].astype(BF16),
                                     row(lru_b_r[o, dd]), row(lru_b_i[o, dd]), row(lru_lam[o, dd]),
                                     reverse=(dd == 1))
                hs.append(h)
                fl.append(h_fin.reshape(bp, dr))
            new_lru.append(jnp.stack(fl, axis=1))
            acts = [hs[0], hs[1], yg]
            small = [row(norm_mix_post[l])]
            body, name, w_out = _tail_odd_kernel, "tail_odd", w_out_odd[o]

        x = _layer_tail(lay, body, name, acts, x, mod_l, small, w_out.astype(BF16),
                        row(norm_mlp_pre[l]), row(norm_mlp_post[l]), w_mlp_in_bf, w_mlp_out_bf, l)

    y_prompt, y_sample = _unembed(lay, x, x_prompt.shape, x_sample.shape)
    y_prompt = y_prompt.astype(x_prompt.dtype)
    y_sample = y_sample.astype(x_sample.dtype)
    p_a = s5_lam_re.shape[3]
    new_s5_re = jnp.stack(new_re, axis=1).reshape(bp, -1, n_dir, ngrp, p_a)
    new_s5_im = jnp.stack(new_im, axis=1).reshape(bp, -1, n_dir, ngrp, p_a)
    return (y_prompt, y_sample, new_s5_re, new_s5_im,
            jnp.stack(new_delta, axis=1), jnp.stack(new_lru, axis=1))
```
